```python
import math
import jax
import jax.numpy as jnp
from jax import lax
import numpy as np

D_MODEL = 2048
BATCH = 16
SEQ = 2048
DEPTH = 4

CTX_LEN = 256
GRID_W = 64
NORM_EPS = 1e-6
N_BRANCH = 4
HY_W = 512
HY_ORDER = 2
HY_SHORT = 3
HY_BANDS = 16
HY_FEAT = 1 + 2 * HY_BANDS
HY_HID = 64
HY_IN = (HY_ORDER + 1) * HY_W
LRU_W = 512
LRU_HEADS = 8
LRU_BW = LRU_W // LRU_HEADS
LRU_CONV = 4
LRU_C = 8.0
LRU_IN = 2 * LRU_W
SSD_HEADS = 8
SSD_HEAD_DIM = 64
SSD_W = SSD_HEADS * SSD_HEAD_DIM
SSD_GROUPS = 2
SSD_STATE = 128
SSD_CONV = 4
SSD_CHUNK = 128
SSD_XBC = SSD_W + 2 * SSD_GROUPS * SSD_STATE
SSD_IN = SSD_W + SSD_XBC + 2 * SSD_HEADS
ATT_HEADS = 8
ATT_KV_HEADS = 2
ATT_HEAD_DIM = 128
ATT_BLOCK = 128
ROPE_THETA = 10000.0
ATT_Q = ATT_HEADS * ATT_HEAD_DIM
ATT_KV = ATT_KV_HEADS * ATT_HEAD_DIM
ATT_IN = ATT_Q + 2 * ATT_KV
OFF_HY = 0
OFF_LRU = OFF_HY + HY_IN
OFF_SSD = OFF_LRU + LRU_IN
OFF_ATT = OFF_SSD + SSD_IN
IN_W = OFF_ATT + ATT_IN
MIX_W = HY_W + LRU_W + SSD_W + ATT_Q
FF_DENSE = 5504
N_EXPERTS = 8
TOP_K = 2
FF_EXPERT = 4096

kernel_name = 'hybrid_gated_mixer_dit_block'


def rms_norm(x, gain):
    xf = x.astype(jnp.float32)
    y = xf * lax.rsqrt(jnp.mean(jnp.square(xf), axis=-1, keepdims=True) + NORM_EPS)
    return (y * gain.astype(jnp.float32)).astype(x.dtype)


def dw_conv(u, w, b, pad_left):
    k, ch = w.shape
    y = lax.conv_general_dilated(u, w[:, None, :].astype(u.dtype), window_strides=(1,),
                                 padding=[(pad_left, k - 1 - pad_left)],
                                 dimension_numbers=('NWC', 'WIO', 'NWC'), feature_group_count=ch)
    return y + b


def linear_scan(a, b, h0):
    b = b.at[:, 0].add(a[:, 0] * h0)
    def combine(l, r):
        return l[0] * r[0], r[0] * l[1] + r[1]
    _, h = lax.associative_scan(combine, (a, b), axis=1)
    return h


def flip_seq(t):
    return jnp.flip(t, axis=1)


def same_seq(t):
    return t


def hyena_filters(L, w1, b1, w2, b2, freq, w3, b3, decay):
    f32 = jnp.float32
    t = jnp.arange(L, dtype=f32)
    tn = t / L
    bands = jnp.linspace(1e-4, HY_BANDS - 1, HY_BANDS, dtype=f32)
    ang = (2.0 * math.pi / L) * t[:, None] * bands[None, :]
    feat = jnp.concatenate([tn[:, None], jnp.cos(ang), -jnp.sin(ang)], axis=-1)
    hdn = jnp.sin(freq[0].astype(f32) * (feat @ w1.astype(f32) + b1.astype(f32)))
    hdn = jnp.sin(freq[1].astype(f32) * (hdn @ w2.astype(f32) + b2.astype(f32)))
    filt = (hdn @ w3.astype(f32) + b3.astype(f32)).reshape(L, HY_ORDER, 2, HY_W)
    window = jnp.exp(-tn[:, None, None, None] * jnp.abs(decay.astype(f32))[None])
    return filt * window


def long_conv_bidir(u, h_fwd, h_bwd, bias):
    L = u.shape[1]
    k = jnp.concatenate([h_fwd, jnp.zeros_like(h_fwd[:1]), h_bwd[:0:-1]], axis=0)
    kf = jnp.fft.rfft(k, n=2 * L, axis=0)
    uf32 = u.astype(jnp.float32)
    uf = jnp.fft.rfft(uf32, n=2 * L, axis=1)
    y = jnp.fft.irfft(uf * kf[None], n=2 * L, axis=1)[:, :L]
    return (y + bias.astype(jnp.float32) * uf32).astype(u.dtype)


def hyena_mixer(pc, pl, conv_w, conv_b, w1, b1, w2, b2, freq, w3, b3, decay, bias, with_ctx):
    def run(p):
        filt = hyena_filters(p.shape[1], w1, b1, w2, b2, freq, w3, b3, decay)
        u = dw_conv(p, conv_w, conv_b, (HY_SHORT - 1) // 2)
        v, x1, x2 = jnp.split(u, HY_ORDER + 1, axis=-1)
        z = x1 * long_conv_bidir(v, filt[:, 0, 0], filt[:, 0, 1], bias[0])
        return x2 * long_conv_bidir(z, filt[:, 1, 0], filt[:, 1, 1], bias[1])
    y_c = run(pc) if with_ctx else None
    return y_c, run(pl)


def rglru_coeffs(xc, wa, ba, wx, bx, lam):
    f32 = jnp.float32
    bsz, L, _ = xc.shape
    xf = xc.astype(f32)
    xh = xf.reshape(bsz, L, LRU_HEADS, LRU_BW)
    r = jax.nn.sigmoid(jnp.einsum('blhi,hij->blhj', xh, wa.astype(f32)) + ba.astype(f32)).reshape(bsz, L, LRU_W)
    i = jax.nn.sigmoid(jnp.einsum('blhi,hij->blhj', xh, wx.astype(f32)) + bx.astype(f32)).reshape(bsz, L, LRU_W)
    log_a = -LRU_C * r * jax.nn.softplus(-lam.astype(f32))
    a = jnp.exp(log_a)
    b = jnp.sqrt(-jnp.expm1(2.0 * log_a)) * i * xf
    return a, b


def rglru_mixer(pc, pl, conv_w, conv_b, wa, ba, wx, bx, lam, with_ctx):
    f32 = jnp.float32
    gate_c, xin_c = jnp.split(pc, 2, axis=-1)
    gate_l, xin_l = jnp.split(pl, 2, axis=-1)
    xc_c = dw_conv(xin_c, conv_w, conv_b, LRU_CONV // 2)
    xc_l = dw_conv(xin_l, conv_w, conv_b, LRU_CONV // 2)
    bsz = pl.shape[0]
    h_ctx, h_lat = [], []
    for d, fl in enumerate((same_seq, flip_seq)):
        a, b = rglru_coeffs(fl(xc_c), wa[d], ba[d], wx[d], bx[d], lam[d])
        hc = linear_scan(a, b, jnp.zeros((bsz, LRU_W), f32))
        a, b = rglru_coeffs(fl(xc_l), wa[d], ba[d], wx[d], bx[d], lam[d])
        hl = linear_scan(a, b, hc[:, -1])
        h_ctx.append(fl(hc))
        h_lat.append(fl(hl))
    y_l = (jax.nn.gelu(gate_l.astype(f32)) * (h_lat[0] + h_lat[1])).astype(pl.dtype)
    y_c = (jax.nn.gelu(gate_c.astype(f32)) * (h_ctx[0] + h_ctx[1])).astype(pc.dtype) if with_ctx else None
    return y_c, y_l


def ssd_scan(x, dt, A, bm, cm, h0):
    bsz, L = x.shape[:2]
    nc = L // SSD_CHUNK
    R = SSD_HEADS // SSD_GROUPS
    shp = (bsz, nc, SSD_CHUNK, SSD_GROUPS, R)
    xdt = (x * dt[..., None]).reshape(*shp, SSD_HEAD_DIM)
    bg = bm.reshape(bsz, nc, SSD_CHUNK, SSD_GROUPS, SSD_STATE)
    cg = cm.reshape(bsz, nc, SSD_CHUNK, SSD_GROUPS, SSD_STATE)
    cs = jnp.cumsum((dt * A).reshape(shp), axis=2)
    causal = jnp.tril(jnp.ones((SSD_CHUNK, SSD_CHUNK), dtype=bool))
    seg = cs[:, :, :, None] - cs[:, :, None, :]
    decay = jnp.exp(jnp.where(causal[:, :, None, None], seg, -jnp.inf))
    cb = jnp.einsum('bcign,bcjgn->bcijg', cg, bg)
    y_diag = jnp.einsum('bcijg,bcijgr,bcjgrp->bcigrp', cb, decay, xdt)
    w_end = jnp.exp(cs[:, :, -1:] - cs)
    states = jnp.einsum('bcjgn,bcjgr,bcjgrp->bcgrpn', bg, w_end, xdt)
    chunk_a = jnp.exp(cs[:, :, -1])[..., None, None]
    s_end = linear_scan(chunk_a, states, h0)
    s_prev = jnp.concatenate([h0[:, None], s_end[:, :-1]], axis=1)
    y_off = jnp.einsum('bcign,bcigr,bcgrpn->bcigrp', cg, jnp.exp(cs), s_prev)
    y = (y_diag + y_off).reshape(bsz, L, SSD_HEADS, SSD_HEAD_DIM)
    return y, s_end[:, -1]


def ssd_mixer(pc, pl, conv_w, conv_b, a_log, dt_bias, d_skip, norm_g, with_ctx):
    f32 = jnp.float32
    def prep(p):
        bsz, L, _ = p.shape
        z, xbc, dt_raw = jnp.split(p, [SSD_W, SSD_W + SSD_XBC], axis=-1)
        xbc = jax.nn.silu(dw_conv(xbc, conv_w, conv_b, SSD_CONV // 2)).astype(f32)
        xs, bm, cm = jnp.split(xbc, [SSD_W, SSD_W + SSD_GROUPS * SSD_STATE], axis=-1)
        return (z, xs.reshape(bsz, L, SSD_HEADS, SSD_HEAD_DIM),
                bm.reshape(bsz, L, SSD_GROUPS, SSD_STATE), cm.reshape(bsz, L, SSD_GROUPS, SSD_STATE),
                dt_raw.astype(f32))
    zc, xc_, bc, cc, dtc = prep(pc)
    zl, xl, bl, cl, dtl = prep(pl)
    bsz = pl.shape[0]
    h0 = jnp.zeros((bsz, SSD_GROUPS, SSD_HEADS // SSD_GROUPS, SSD_HEAD_DIM, SSD_STATE), f32)
    ys_c, ys_l = [], []
    for d, fl in enumerate((same_seq, flip_seq)):
        A = -jnp.exp(a_log[d].astype(f32))
        dtb = dt_bias[d].astype(f32)
        dt_c = jax.nn.softplus(dtc[..., d * SSD_HEADS:(d + 1) * SSD_HEADS] + dtb)
        dt_l = jax.nn.softplus(dtl[..., d * SSD_HEADS:(d + 1) * SSD_HEADS] + dtb)
        yc, s_ctx = ssd_scan(fl(xc_), fl(dt_c), A, fl(bc), fl(cc), h0)
        yl, _ = ssd_scan(fl(xl), fl(dt_l), A, fl(bl), fl(cl), s_ctx)
        ys_c.append(fl(yc))
        ys_l.append(fl(yl))
    def finish(ys, xs, z):
        bsz_, L = xs.shape[:2]
        y = ys[0] + ys[1] + d_skip.astype(f32)[:, None] * xs
        y = y.reshape(bsz_, L, SSD_W) * jax.nn.silu(z.astype(f32))
        return rms_norm(y, norm_g).astype(z.dtype)
    y_c = finish(ys_c, xc_, zc) if with_ctx else None
    return y_c, finish(ys_l, xl, zl)


def axial_rope_tables(row_id, col_id):
    axis_dim = ATT_HEAD_DIM // 2
    inv = ROPE_THETA ** (-jnp.arange(0, axis_dim, 2, dtype=jnp.float32) / axis_dim)
    ang = jnp.concatenate([row_id[:, None] * inv, col_id[:, None] * inv], axis=-1)
    return jnp.cos(ang), jnp.sin(ang)


def apply_rope(x, cos, sin):
    xf = x.astype(jnp.float32).reshape(*x.shape[:-1], ATT_HEAD_DIM // 2, 2)
    x1, x2 = xf[..., 0], xf[..., 1]
    c = cos[None, :, None]
    s = sin[None, :, None]
    out = jnp.stack([x1 * c - x2 * s, x1 * s + x2 * c], axis=-1).reshape(x.shape)
    return out.astype(x.dtype)


def attend(q, k, v):
    s = jnp.einsum('bqkgd,bskd->bkgqs', q.astype(jnp.float32), k.astype(jnp.float32)) * (ATT_HEAD_DIM ** -0.5)
    p = jax.nn.softmax(s, axis=-1)
    return jnp.einsum('bkgqs,bskd->bqkgd', p, v.astype(jnp.float32)).astype(v.dtype)


def q_heads(pq, gain):
    bsz, L, _ = pq.shape
    return rms_norm(pq.reshape(bsz, L, ATT_HEADS, ATT_HEAD_DIM), gain)


def kv_heads(pkv, gain):
    bsz, L, _ = pkv.shape
    k, v = jnp.split(pkv, 2, axis=-1)
    k = rms_norm(k.reshape(bsz, L, ATT_KV_HEADS, ATT_HEAD_DIM), gain)
    return k, v.reshape(bsz, L, ATT_KV_HEADS, ATT_HEAD_DIM)


def attention_mixer(pc, pl, q_norm, k_norm, cos, sin, with_ctx):
    G = ATT_HEADS // ATT_KV_HEADS
    bsz, L, _ = pl.shape
    Lc = pc.shape[1]
    kc, vc = kv_heads(pc[..., ATT_Q:], k_norm)
    kl, vl = kv_heads(pl[..., ATT_Q:], k_norm)
    ql = apply_rope(q_heads(pl[..., :ATT_Q], q_norm), cos, sin)
    kl = apply_rope(kl, cos, sin)
    k_all = jnp.concatenate([kc, kl], axis=1)
    v_all = jnp.concatenate([vc, vl], axis=1)
    nb = L // ATT_BLOCK
    qb = ql.reshape(bsz, nb, ATT_BLOCK, ATT_KV_HEADS, G, ATT_HEAD_DIM).swapaxes(0, 1)
    ob = lax.map(lambda qi: attend(qi, k_all, v_all), qb)
    y_l = ob.swapaxes(0, 1).reshape(bsz, L, ATT_Q)
    y_c = None
    if with_ctx:
        qc = q_heads(pc[..., :ATT_Q], q_norm).reshape(bsz, Lc, ATT_KV_HEADS, G, ATT_HEAD_DIM)
        y_c = attend(qc, kc, vc).reshape(bsz, Lc, ATT_Q)
    return y_c, y_l


def merge_branches(h, ys, w_branch, w_gate, b_gate, w_out):
    merged = None
    off = 0
    for i, y in enumerate(ys):
        width = y.shape[-1]
        gate = jax.nn.sigmoid(h @ w_gate[i] + b_gate[i])
        term = gate * (y @ w_branch[off:off + width])
        merged = term if merged is None else merged + term
        off += width
    return merged @ w_out


def swiglu(h, w1, w3, w2):
    return (jax.nn.silu(h @ w1) * (h @ w3)) @ w2


def moe_swiglu(h, w_router, b_router, w1, w3, w2):
    bsz, L, D = h.shape
    t = h.reshape(bsz * L, D)
    logits = (t @ w_router + b_router).astype(jnp.float32)
    top_v, top_i = lax.top_k(logits, TOP_K)
    top_w = jax.nn.softmax(top_v, axis=-1)
    gates = jnp.einsum('tk,tke->te', top_w, jax.nn.one_hot(top_i, N_EXPERTS, dtype=jnp.float32)).astype(h.dtype)
    out = None
    for e in range(N_EXPERTS):
        y_e = gates[:, e:e + 1] * swiglu(t, w1[e], w3[e], w2[e])
        out = y_e if out is None else out + y_e
    return out.reshape(bsz, L, D)


def setup_inputs(seed: int = 0) -> dict:
    key = jax.random.key(seed)
    keys = iter(jax.random.split(key, 64))
    f32 = jnp.float32
    D = D_MODEL
    n_dense = (DEPTH + 1) // 2
    n_moe = DEPTH // 2
    def nrm(shape, scale):
        return jax.random.normal(next(keys), shape, f32) * scale
    def gain(shape, noise=0.05):
        return 1.0 + nrm(shape, noise)
    def unif(shape, lo, hi):
        return jax.random.uniform(next(keys), shape, f32, lo, hi)
    inp = {}
    inp['x'] = nrm((BATCH, SEQ, D), 1.0)
    inp['c'] = nrm((BATCH, D), 1.0)
    inp['ctx'] = nrm((BATCH, CTX_LEN, D), 1.0)
    inp['c_ctx'] = nrm((D,), 1.0)
    inp['norm_mix'] = gain((DEPTH, D))
    inp['norm_ffn'] = gain((DEPTH, D))
    inp['ada_w'] = nrm((DEPTH, D, 6 * D), 0.5 * D ** -0.5)
    inp['ada_b'] = nrm((DEPTH, 6 * D), 0.02)
    inp['w_in'] = nrm((DEPTH, D, IN_W), D ** -0.5)
    inp['hy_conv_w'] = nrm((DEPTH, HY_SHORT, HY_IN), HY_SHORT ** -0.5)
    inp['hy_conv_b'] = nrm((DEPTH, HY_IN), 0.02)
    inp['hy_w1'] = nrm((DEPTH, HY_FEAT, HY_HID), HY_FEAT ** -0.5)
    inp['hy_b1'] = nrm((DEPTH, HY_HID), 0.1)
    inp['hy_w2'] = nrm((DEPTH, HY_HID, HY_HID), HY_HID ** -0.5)
    inp['hy_b2'] = nrm((DEPTH, HY_HID), 0.1)
    inp['hy_freq'] = gain((DEPTH, 2, HY_HID), 0.1)
    inp['hy_w3'] = nrm((DEPTH, HY_HID, HY_ORDER * 2 * HY_W), 0.05 * HY_HID ** -0.5)
    inp['hy_b3'] = nrm((DEPTH, HY_ORDER * 2 * HY_W), 0.001)
    base_decay = jnp.linspace(math.log(1e-2) / 0.3, math.log(1e-2) / 1.5, HY_W, dtype=f32)
    inp['hy_decay'] = base_decay + nrm((DEPTH, HY_ORDER, 2, HY_W), 0.1)
    inp['hy_bias'] = nrm((DEPTH, HY_ORDER, HY_W), 0.5)
    inp['lru_conv_w'] = nrm((DEPTH, LRU_CONV, LRU_W), 0.5)
    inp['lru_conv_b'] = nrm((DEPTH, LRU_W), 0.02)
    inp['lru_wa'] = nrm((DEPTH, 2, LRU_HEADS, LRU_BW, LRU_BW), LRU_BW ** -0.5)
    inp['lru_ba'] = nrm((DEPTH, 2, LRU_HEADS, LRU_BW), 0.1)
    inp['lru_wx'] = nrm((DEPTH, 2, LRU_HEADS, LRU_BW, LRU_BW), LRU_BW ** -0.5)
    inp['lru_bx'] = nrm((DEPTH, 2, LRU_HEADS, LRU_BW), 0.1)
    a_root = unif((DEPTH, 2, LRU_W), 0.9, 0.999) ** (1.0 / LRU_C)
    inp['lru_lambda'] = jnp.log(a_root) - jnp.log1p(-a_root)
    inp['ssd_conv_w'] = nrm((DEPTH, SSD_CONV, SSD_XBC), 0.5)
    inp['ssd_conv_b'] = nrm((DEPTH, SSD_XBC), 0.02)
    inp['ssd_a_log'] = jnp.log(unif((DEPTH, 2, SSD_HEADS), 1.0, 16.0))
    dt0 = jnp.exp(unif((DEPTH, 2, SSD_HEADS), math.log(1e-3), math.log(1e-1)))
    inp['ssd_dt_bias'] = dt0 + jnp.log(-jnp.expm1(-dt0))
    inp['ssd_d'] = gain((DEPTH, SSD_HEADS), 0.1)
    inp['ssd_norm'] = gain((DEPTH, SSD_W))
    inp['att_q_norm'] = gain((DEPTH, ATT_HEAD_DIM))
    inp['att_k_norm'] = gain((DEPTH, ATT_HEAD_DIM))
    inp['w_branch'] = nrm((DEPTH, MIX_W, D), (MIX_W // N_BRANCH) ** -0.5)
    inp['w_gate'] = nrm((DEPTH, N_BRANCH, D, D), D ** -0.5)
    inp['b_gate'] = nrm((DEPTH, N_BRANCH, D), 0.1)
    inp['w_out'] = nrm((DEPTH, D, D), D ** -0.5)
    inp['ffn_w1'] = nrm((n_dense, D, FF_DENSE), D ** -0.5)
    inp['ffn_w3'] = nrm((n_dense, D, FF_DENSE), D ** -0.5)
    inp['ffn_w2'] = nrm((n_dense, FF_DENSE, D), FF_DENSE ** -0.5)
    inp['moe_router'] = nrm((n_moe, D, N_EXPERTS), D ** -0.5)
    inp['moe_router_b'] = nrm((n_moe, N_EXPERTS), 0.01)
    inp['moe_w1'] = nrm((n_moe, N_EXPERTS, D, FF_EXPERT), D ** -0.5)
    inp['moe_w3'] = nrm((n_moe, N_EXPERTS, D, FF_EXPERT), D ** -0.5)
    inp['moe_w2'] = nrm((n_moe, N_EXPERTS, FF_EXPERT, D), FF_EXPERT ** -0.5)
    inp['norm_final'] = gain((D,))
    return inp


def reference(x, c, ctx, c_ctx, norm_mix, norm_ffn, ada_w, ada_b, w_in,
              hy_conv_w, hy_conv_b, hy_w1, hy_b1, hy_w2, hy_b2, hy_freq, hy_w3, hy_b3, hy_decay, hy_bias,
              lru_conv_w, lru_conv_b, lru_wa, lru_ba, lru_wx, lru_bx, lru_lambda,
              ssd_conv_w, ssd_conv_b, ssd_a_log, ssd_dt_bias, ssd_d, ssd_norm,
              att_q_norm, att_k_norm, w_branch, w_gate, b_gate, w_out,
              ffn_w1, ffn_w3, ffn_w2, moe_router, moe_router_b, moe_w1, moe_w3, moe_w2, norm_final):
    Lc = ctx.shape[1]
    L = x.shape[1]
    ROWS = L // GRID_W
    row_id = jnp.repeat(jnp.arange(ROWS, dtype=jnp.float32), GRID_W)
    col_id = jnp.tile(jnp.arange(GRID_W, dtype=jnp.float32), ROWS)
    cos, sin = axial_rope_tables(row_id, col_id)
    s_c = jax.nn.silu(c)
    s_ctx = jax.nn.silu(c_ctx)
    xc = ctx
    for l in range(DEPTH):
        with_ctx = l < DEPTH - 1
        mod_l = jnp.split((s_c @ ada_w[l] + ada_b[l])[:, None, :], 6, axis=-1)
        mod_c = jnp.split((s_ctx @ ada_w[l] + ada_b[l])[None, None, :], 6, axis=-1)
        h_l = rms_norm(x, norm_mix[l]) * (1.0 + mod_l[1]) + mod_l[0]
        h_c = rms_norm(xc, norm_mix[l]) * (1.0 + mod_c[1]) + mod_c[0]
        h_all = jnp.concatenate([h_c, h_l], axis=1)
        proj = h_all @ w_in[l]
        pc, pl = proj[:, :Lc], proj[:, Lc:]
        hy_c, hy_l = hyena_mixer(pc[..., OFF_HY:OFF_LRU], pl[..., OFF_HY:OFF_LRU], hy_conv_w[l], hy_conv_b[l],
                                 hy_w1[l], hy_b1[l], hy_w2[l], hy_b2[l], hy_freq[l], hy_w3[l], hy_b3[l],
                                 hy_decay[l], hy_bias[l], with_ctx)
        lru_c, lru_l = rglru_mixer(pc[..., OFF_LRU:OFF_SSD], pl[..., OFF_LRU:OFF_SSD], lru_conv_w[l], lru_conv_b[l],
                                   lru_wa[l], lru_ba[l], lru_wx[l], lru_bx[l], lru_lambda[l], with_ctx)
        ssd_c, ssd_l = ssd_mixer(pc[..., OFF_SSD:OFF_ATT], pl[..., OFF_SSD:OFF_ATT], ssd_conv_w[l], ssd_conv_b[l],
                                 ssd_a_log[l], ssd_dt_bias[l], ssd_d[l], ssd_norm[l], with_ctx)
        att_c, att_l = attention_mixer(pc[..., OFF_ATT:IN_W], pl[..., OFF_ATT:IN_W], att_q_norm[l], att_k_norm[l],
                                       cos, sin, with_ctx)
        ys_l = [hy_l, lru_l, ssd_l, att_l]
        if with_ctx:
            ys_c = [hy_c, lru_c, ssd_c, att_c]
            ys = [jnp.concatenate([yc_, yl_], axis=1) for yc_, yl_ in zip(ys_c, ys_l)]
            m = merge_branches(h_all, ys, w_branch[l], w_gate[l], b_gate[l], w_out[l])
            xc = xc + mod_c[2] * m[:, :Lc]
            x = x + mod_l[2] * m[:, Lc:]
        else:
            x = x + mod_l[2] * merge_branches(h_l, ys_l, w_branch[l], w_gate[l], b_gate[l], w_out[l])
        f_l = rms_norm(x, norm_ffn[l]) * (1.0 + mod_l[4]) + mod_l[3]
        if with_ctx:
            f_c = rms_norm(xc, norm_ffn[l]) * (1.0 + mod_c[4]) + mod_c[3]
            f_in = jnp.concatenate([f_c, f_l], axis=1)
        else:
            f_in = f_l
        j = l // 2
        if l % 2 == 0:
            f_out = swiglu(f_in, ffn_w1[j], ffn_w3[j], ffn_w2[j])
        else:
            f_out = moe_swiglu(f_in, moe_router[j], moe_router_b[j], moe_w1[j], moe_w3[j], moe_w2[j])
        if with_ctx:
            xc = xc + mod_c[5] * f_out[:, :Lc]
            x = x + mod_l[5] * f_out[:, Lc:]
        else:
            x = x + mod_l[5] * f_out
    return rms_norm(x, norm_final)
```

```python
import functools
import math

import jax
import jax.numpy as jnp
from jax import lax
from jax.experimental import pallas as pl
from jax.experimental.pallas import tpu as pltpu

f32 = jnp.float32
bf16 = jnp.bfloat16

D_MODEL = 2048
DEPTH = 4
GRID_W = 64
NORM_EPS = 1e-6
HY_W = 512
HY_ORDER = 2
HY_SHORT = 3
HY_BANDS = 16
HY_IN = (HY_ORDER + 1) * HY_W
LRU_W = 512
LRU_HEADS = 8
LRU_BW = LRU_W // LRU_HEADS
LRU_CONV = 4
LRU_C = 8.0
SSD_HEADS = 8
SSD_HEAD_DIM = 64
SSD_W = SSD_HEADS * SSD_HEAD_DIM
SSD_GROUPS = 2
SSD_STATE = 128
SSD_CONV = 4
SSD_CHUNK = 128
SSD_XBC = SSD_W + 2 * SSD_GROUPS * SSD_STATE
ATT_HEADS = 8
ATT_KV_HEADS = 2
ATT_HEAD_DIM = 128
ROPE_THETA = 10000.0
ATT_Q = ATT_HEADS * ATT_HEAD_DIM
ATT_KV = ATT_KV_HEADS * ATT_HEAD_DIM
FF_DENSE = 5504
N_EXPERTS = 8
TOP_K = 2

LANES = 128
SUBLANES = 8
MIB = 2 ** 20

C_HY = 0
C_LRU_G = C_HY + HY_IN
C_LRU_X = C_LRU_G + LRU_W
C_SSD_Z = C_LRU_X + LRU_W
C_SSD_XBC = C_SSD_Z + SSD_W
C_Q = C_SSD_XBC + SSD_XBC
C_K = C_Q + ATT_Q
C_V = C_K + ATT_KV
C_DT = C_V + ATT_KV
PROJ_W = C_DT + LANES
FF_DENSE_PAD = 5632
GATE_LANES = LANES


def _params(sem, vmem_mib):
    return pltpu.CompilerParams(dimension_semantics=sem, vmem_limit_bytes=vmem_mib * MIB)


def _pick(n, cands):
    for c in cands:
        if n % c == 0:
            return c
    raise ValueError(f"no tile for {n} in {cands}")


def _silu(v):
    return v * jax.nn.sigmoid(v)


def _softplus(v):
    return jnp.maximum(v, 0.0) + jnp.log1p(jnp.exp(-jnp.abs(v)))


def _mods_kernel(c_ref, w_ref, b_ref, o_ref):
    s = _silu(c_ref[...]).astype(bf16)
    o_ref[0] = jnp.dot(s, w_ref[0].astype(bf16), preferred_element_type=f32) + b_ref[0]


def _mods(cvec, ada_w, ada_b):
    depth, d, n = ada_w.shape
    rows = cvec.shape[0]
    tn = 1024
    return pl.pallas_call(
        _mods_kernel,
        out_shape=jax.ShapeDtypeStruct((depth, rows, n), f32),
        grid=(depth, n // tn),
        in_specs=[pl.BlockSpec((rows, d), lambda l, j: (0, 0)),
                  pl.BlockSpec((1, d, tn), lambda l, j: (l, 0, j)),
                  pl.BlockSpec((1, 1, tn), lambda l, j: (l, 0, j))],
        out_specs=pl.BlockSpec((1, rows, tn), lambda l, j: (l, 0, j)),
        compiler_params=_params(("parallel", "parallel"), 40),
        name="mods",
    )(cvec, ada_w, ada_b.reshape(depth, 1, n))


def _row_mod(ml_ref, mc_ref, k, is_ctx):
    return jnp.where(is_ctx, mc_ref[0, k:k + 1, :], ml_ref[0, k:k + 1, :])


def _is_ctx_rows(tm, tpb, l_lat, axis):
    t = (pl.program_id(axis) % tpb) * tm + lax.broadcasted_iota(jnp.int32, (tm, 1), 0)
    return t >= l_lat


def _normmod_kernel(x_ref, g_ref, ml_ref, mc_ref, o_ref, *, tm, tpb, l_lat, k_shift, k_scale):
    x = x_ref[...]
    y = x * lax.rsqrt(jnp.mean(x * x, axis=-1, keepdims=True) + NORM_EPS) * g_ref[...]
    is_ctx = _is_ctx_rows(tm, tpb, l_lat, 0)
    shift = _row_mod(ml_ref, mc_ref, k_shift, is_ctx)
    scale = _row_mod(ml_ref, mc_ref, k_scale, is_ctx)
    o_ref[...] = (y * (1.0 + scale) + shift).astype(o_ref.dtype)


def _normmod(x2, gain, modtab, nb, t_all, l_lat, k_shift, k_scale):
    r, d = x2.shape
    tm = _pick(t_all, (768, 384, 256, 128))
    tpb = t_all // tm
    return pl.pallas_call(
        functools.partial(_normmod_kernel, tm=tm, tpb=tpb, l_lat=l_lat, k_shift=k_shift, k_scale=k_scale),
        out_shape=jax.ShapeDtypeStruct((r, d), bf16),
        grid=(r // tm,),
        in_specs=[pl.BlockSpec((tm, d), lambda i: (i, 0)),
                  pl.BlockSpec((1, d), lambda i: (0, 0)),
                  pl.BlockSpec((1, 6, d), lambda i: (i // tpb, 0, 0)),
                  pl.BlockSpec((1, 6, d), lambda i: (nb, 0, 0))],
        out_specs=pl.BlockSpec((tm, d), lambda i: (i, 0)),
        compiler_params=_params(("parallel",), 40),
        name="normmod",
    )(x2, gain.reshape(1, d), modtab, modtab)


def _mm_kernel(a_ref, w_ref, o_ref):
    o_ref[...] = jnp.dot(a_ref[...], w_ref[...], preferred_element_type=f32).astype(o_ref.dtype)


def _mm(a, w, out_dtype, tm, tn, vmem=48):
    m, k = a.shape
    n = w.shape[1]
    return pl.pallas_call(
        _mm_kernel,
        out_shape=jax.ShapeDtypeStruct((m, n), out_dtype),
        grid=(n // tn, m // tm),
        in_specs=[pl.BlockSpec((tm, k), lambda j, i: (i, 0)),
                  pl.BlockSpec((k, tn), lambda j, i: (0, j))],
        out_specs=pl.BlockSpec((tm, tn), lambda j, i: (i, j)),
        compiler_params=_params(("parallel", "parallel"), vmem),
        name="mm",
    )(a, w)


def _mm_swiglu_kernel(a_ref, w1_ref, w3_ref, o_ref):
    a = a_ref[...]
    g = jnp.dot(a, w1_ref[...], preferred_element_type=f32)
    u = jnp.dot(a, w3_ref[...], preferred_element_type=f32)
    o_ref[...] = (_silu(g) * u).astype(o_ref.dtype)


def _mm_swiglu(a, w1, w3, tm, tn, vmem=48):
    m, k = a.shape
    n = w1.shape[1]
    wspec = pl.BlockSpec((k, tn), lambda j, i: (0, j))
    return pl.pallas_call(
        _mm_swiglu_kernel,
        out_shape=jax.ShapeDtypeStruct((m, n), bf16),
        grid=(n // tn, m // tm),
        in_specs=[pl.BlockSpec((tm, k), lambda j, i: (i, 0)), wspec, wspec],
        out_specs=pl.BlockSpec((tm, tn), lambda j, i: (i, j)),
        compiler_params=_params(("parallel", "parallel"), vmem),
        name="mm_swiglu",
    )(a, w1, w3)


def _mm_resid_kernel(*refs, tm, tpb, l_lat, k_mod, gate_col):
    if gate_col is None:
        a_ref, w_ref, x_ref, ml_ref, mc_ref, o_ref = refs
    else:
        a_ref, w_ref, x_ref, ml_ref, mc_ref, g_ref, o_ref = refs
    acc = jnp.dot(a_ref[...], w_ref[...], preferred_element_type=f32)
    mod = _row_mod(ml_ref, mc_ref, k_mod, _is_ctx_rows(tm, tpb, l_lat, 1))
    if gate_col is not None:
        acc = acc * g_ref[:, gate_col:gate_col + 1]
    o_ref[...] = x_ref[...] + mod * acc


def _mm_resid(a, w, x2, modtab, nb, t_all, l_lat, k_mod, tn, gates=None, gate_col=None, vmem=48):
    m, k = a.shape
    n = w.shape[1]
    tm = _pick(t_all, (768, 384, 256, 128))
    tpb = t_all // tm
    in_specs = [pl.BlockSpec((tm, k), lambda j, i: (i, 0)),
                pl.BlockSpec((k, tn), lambda j, i: (0, j)),
                pl.BlockSpec((tm, tn), lambda j, i: (i, j)),
                pl.BlockSpec((1, 6, tn), lambda j, i: (i // tpb, 0, j)),
                pl.BlockSpec((1, 6, tn), lambda j, i: (nb, 0, j))]
    args = [a, w, x2, modtab, modtab]
    if gates is not None:
        in_specs.append(pl.BlockSpec((tm, GATE_LANES), lambda j, i: (i, 0)))
        args.append(gates)
    return pl.pallas_call(
        functools.partial(_mm_resid_kernel, tm=tm, tpb=tpb, l_lat=l_lat, k_mod=k_mod,
                          gate_col=gate_col if gates is not None else None),
        out_shape=jax.ShapeDtypeStruct((m, n), f32),
        grid=(n // tn, m // tm),
        in_specs=in_specs,
        out_specs=pl.BlockSpec((tm, tn), lambda j, i: (i, j)),
        input_output_aliases={2: 0},
        compiler_params=_params(("parallel", "parallel"), vmem),
        name="mm_resid",
    )(*args)


def _merge_kernel(h_ref, y0_ref, y1_ref, y2_ref, y3_ref, wg_ref, bg_ref,
                  wb0_ref, wb1_ref, wb2_ref, wb3_ref, o_ref):
    h = h_ref[...]
    acc = None
    for i, (y_ref, wb_ref) in enumerate(((y0_ref, wb0_ref), (y1_ref, wb1_ref),
                                         (y2_ref, wb2_ref), (y3_ref, wb3_ref))):
        g = jnp.dot(h, wg_ref[i], preferred_element_type=f32) + bg_ref[i]
        p = jnp.dot(y_ref[...], wb_ref[...], preferred_element_type=f32)
        term = jax.nn.sigmoid(g) * p
        acc = term if acc is None else acc + term
    o_ref[...] = acc.astype(o_ref.dtype)


def _merge(h, ys, wg, bg, wbs, tm, tn=256):
    m, d = h.shape
    n = wg.shape[2]
    in_specs = [pl.BlockSpec((tm, d), lambda j, i: (i, 0))]
    in_specs += [pl.BlockSpec((tm, y.shape[1]), lambda j, i: (i, 0)) for y in ys]
    in_specs += [pl.BlockSpec((4, d, tn), lambda j, i: (0, 0, j)),
                 pl.BlockSpec((4, 1, tn), lambda j, i: (0, 0, j))]
    in_specs += [pl.BlockSpec((wb.shape[0], tn), lambda j, i: (0, j)) for wb in wbs]
    return pl.pallas_call(
        _merge_kernel,
        out_shape=jax.ShapeDtypeStruct((m, n), bf16),
        grid=(n // tn, m // tm),
        in_specs=in_specs,
        out_specs=pl.BlockSpec((tm, tn), lambda j, i: (i, j)),
        compiler_params=_params(("parallel", "parallel"), 48),
        name="merge",
    )(h, *ys, wg, bg.reshape(4, 1, n), *wbs)


def _norm_rope(x, gain, cos, sin):
    y = x * lax.rsqrt(jnp.mean(x * x, axis=-1, keepdims=True) + NORM_EPS) * gain
    lane = lax.broadcasted_iota(jnp.int32, y.shape, 1)
    partner = jnp.where(lane % 2 == 0, pltpu.roll(y, ATT_HEAD_DIM - 1, 1), pltpu.roll(y, 1, 1))
    return y * cos + partner * sin


def _attn_kernel(q_ref, k_ref, v_ref, cos_ref, sin_ref, qg_ref, kg_ref, o_ref, ks_ref, vs_ref,
                 *, tq, l_lat, n_lat_blocks):
    qi = pl.program_id(2)
    grp = ATT_HEADS // ATT_KV_HEADS
    hd = ATT_HEAD_DIM

    @pl.when(qi == 0)
    def _():
        ks_ref[...] = _norm_rope(k_ref[0], kg_ref[...], cos_ref[...], sin_ref[...]).astype(bf16)
        vs_ref[...] = v_ref[0].astype(bf16)

    r0 = pl.multiple_of(qi * tq, tq)
    cos = cos_ref[pl.ds(r0, tq), :]
    sin = sin_ref[pl.ds(r0, tq), :]
    q = q_ref[0]
    qs = jnp.concatenate(
        [_norm_rope(q[:, h * hd:(h + 1) * hd], qg_ref[...], cos, sin) for h in range(grp)], axis=0)
    qs = (qs * (hd ** -0.5)).astype(bf16)

    def attend(ks, vs):
        s = lax.dot_general(qs, ks, (((1,), (1,)), ((), ())), preferred_element_type=f32)
        p = jnp.exp(s - jnp.max(s, axis=-1, keepdims=True))
        l = jnp.sum(p, axis=-1, keepdims=True)
        o = jnp.dot(p.astype(bf16), vs, preferred_element_type=f32) / l
        for h in range(grp):
            o_ref[0, :, h * hd:(h + 1) * hd] = o[h * tq:(h + 1) * tq].astype(o_ref.dtype)

    @pl.when(qi < n_lat_blocks)
    def _():
        attend(ks_ref[...], vs_ref[...])

    @pl.when(qi >= n_lat_blocks)
    def _():
        attend(ks_ref[l_lat:, :], vs_ref[l_lat:, :])


def _attention(proj, cos2, sin2, q_gain, k_gain, l_lat):
    nb, t_all, _ = proj.shape
    lc = t_all - l_lat
    tq = lc if (l_lat % lc == 0 and lc <= 256) else _pick(math.gcd(l_lat, lc), (256, 128))
    grp = ATT_HEADS // ATT_KV_HEADS
    qw = grp * ATT_HEAD_DIM
    hd = ATT_HEAD_DIM
    return pl.pallas_call(
        functools.partial(_attn_kernel, tq=tq, l_lat=l_lat, n_lat_blocks=l_lat // tq),
        out_shape=jax.ShapeDtypeStruct((nb, t_all, ATT_Q), bf16),
        grid=(nb, ATT_KV_HEADS, t_all // tq),
        in_specs=[pl.BlockSpec((1, tq, qw), lambda b, g, i: (b, i, C_Q // qw + g)),
                  pl.BlockSpec((1, t_all, hd), lambda b, g, i: (b, 0, C_K // hd + g)),
                  pl.BlockSpec((1, t_all, hd), lambda b, g, i: (b, 0, C_V // hd + g)),
                  pl.BlockSpec((t_all, hd), lambda b, g, i: (0, 0)),
                  pl.BlockSpec((t_all, hd), lambda b, g, i: (0, 0)),
                  pl.BlockSpec((1, hd), lambda b, g, i: (0, 0)),
                  pl.BlockSpec((1, hd), lambda b, g, i: (0, 0))],
        out_specs=pl.BlockSpec((1, tq, qw), lambda b, g, i: (b, i, g)),
        scratch_shapes=[pltpu.VMEM((t_all, hd), bf16), pltpu.VMEM((t_all, hd), bf16)],
        compiler_params=_params(("parallel", "parallel", "arbitrary"), 56),
        name="attention",
    )(proj, proj, proj, cos2, sin2, q_gain.reshape(1, hd), k_gain.reshape(1, hd))


def _rope_tables(l_lat, lc):
    axis_dim = ATT_HEAD_DIM // 2
    rows = l_lat // GRID_W
    row_id = jnp.repeat(jnp.arange(rows, dtype=f32), GRID_W)
    col_id = jnp.tile(jnp.arange(GRID_W, dtype=f32), rows)
    inv = ROPE_THETA ** (-jnp.arange(0, axis_dim, 2, dtype=f32) / axis_dim)
    ang = jnp.concatenate([row_id[:, None] * inv, col_id[:, None] * inv], axis=-1)
    cos = jnp.repeat(jnp.cos(ang), 2, axis=-1)
    sin = jnp.repeat(jnp.sin(ang), 2, axis=-1)
    sign = jnp.tile(jnp.array([-1.0, 1.0], f32), ATT_HEAD_DIM // 2)
    cos2 = jnp.concatenate([cos, jnp.ones((lc, ATT_HEAD_DIM), f32)], axis=0)
    sin2 = jnp.concatenate([sin * sign, jnp.zeros((lc, ATT_HEAD_DIM), f32)], axis=0)
    return cos2, sin2


def _dwconv_kernel(u_ref, w_ref, b_ref, o_ref, *, ksz, pad_left, l_lat, t_all, act):
    u = u_ref[0]
    t = lax.broadcasted_iota(jnp.int32, (t_all, 1), 0)
    is_ctx = t >= l_lat
    t_loc = jnp.where(is_ctx, t - l_lat, t)
    seg_len = jnp.where(is_ctx, t_all - l_lat, l_lat)
    acc = None
    for k in range(ksz):
        s = k - pad_left
        if s == 0:
            term = u
        else:
            shifted = pltpu.roll(u, (t_all - s) % t_all, 0)
            term = jnp.where((t_loc + s >= 0) & (t_loc + s < seg_len), shifted, 0.0)
        term = term * w_ref[k:k + 1, :]
        acc = term if acc is None else acc + term
    acc = acc + b_ref[...]
    if act:
        acc = _silu(acc)
    o_ref[0] = acc.astype(o_ref.dtype)


def _dwconv(proj, col0, width, w, b, pad_left, l_lat, act=False):
    nb, t_all, _ = proj.shape
    ksz = w.shape[0]
    cw = 256
    return pl.pallas_call(
        functools.partial(_dwconv_kernel, ksz=ksz, pad_left=pad_left, l_lat=l_lat, t_all=t_all, act=act),
        out_shape=jax.ShapeDtypeStruct((nb, t_all, width), f32),
        grid=(nb, width // cw),
        in_specs=[pl.BlockSpec((1, t_all, cw), lambda bi, c: (bi, 0, col0 // cw + c)),
                  pl.BlockSpec((ksz, cw), lambda bi, c: (0, c)),
                  pl.BlockSpec((1, cw), lambda bi, c: (0, c))],
        out_specs=pl.BlockSpec((1, t_all, cw), lambda bi, c: (bi, 0, c)),
        compiler_params=_params(("parallel", "parallel"), 48),
        name="dwconv",
    )(proj, w, b.reshape(1, width))


def _hyconv_kernel(u_ref, g_ref, m_ref, mi_ref, k_ref, o_ref, ub_ref, acc_ref, *, fc, nf):
    f = pl.program_id(2)

    @pl.when(f == 0)
    def _():
        ub_ref[...] = u_ref[0].astype(bf16)
        acc_ref[...] = jnp.zeros_like(acc_ref)

    spec = jnp.dot(m_ref[0], ub_ref[...], preferred_element_type=f32)
    a, bv = spec[:fc], spec[fc:]
    kk = k_ref[0]
    kr, ki = kk[:fc], kk[fc:]
    y = jnp.concatenate([a * kr + bv * ki, a * ki - bv * kr], axis=0).astype(bf16)
    acc_ref[...] += jnp.dot(mi_ref[0], y, preferred_element_type=f32)

    @pl.when(f == nf - 1)
    def _():
        o_ref[0] = (acc_ref[...] * g_ref[0]).astype(o_ref.dtype)


def _hyconv(u, u_blk, gate, gate_blk, lseg, dft, kspec):
    m_mat, mi_mat = dft
    nf, fc2, _ = m_mat.shape
    fc = fc2 // 2
    nb = u.shape[0]
    cw = HY_W
    return pl.pallas_call(
        functools.partial(_hyconv_kernel, fc=fc, nf=nf),
        out_shape=jax.ShapeDtypeStruct((nb, lseg, cw), bf16),
        grid=(nb, 1, nf),
        in_specs=[pl.BlockSpec((1, lseg, cw), lambda b, c, f: (b,) + u_blk),
                  pl.BlockSpec((1, lseg, cw), lambda b, c, f: (b,) + gate_blk),
                  pl.BlockSpec((1, fc2, lseg), lambda b, c, f: (f, 0, 0)),
                  pl.BlockSpec((1, lseg, fc2), lambda b, c, f: (f, 0, 0)),
                  pl.BlockSpec((1, fc2, cw), lambda b, c, f: (f, 0, 0))],
        out_specs=pl.BlockSpec((1, lseg, cw), lambda b, c, f: (b, 0, 0)),
        scratch_shapes=[pltpu.VMEM((lseg, cw), bf16), pltpu.VMEM((lseg, cw), f32)],
        compiler_params=_params(("parallel", "arbitrary", "arbitrary"), 52),
        name="hyconv",
    )(u, gate, m_mat, mi_mat, kspec)


def _dft_cs(lseg):
    n = 2 * lseg
    f = jnp.arange(lseg, dtype=jnp.int32)[:, None]
    s = jnp.arange(lseg, dtype=jnp.int32)[None, :]
    ph = ((2 * f + 1) * s) % (2 * n)
    ang = ph.astype(f32) * (math.pi / n)
    return jnp.cos(ang), jnp.sin(ang)


def _dft_mats(lseg):
    fc = min(256, lseg // 2)
    nf = lseg // fc
    c, s = _dft_cs(lseg)
    c3 = c.reshape(nf, fc, lseg)
    s3 = s.reshape(nf, fc, lseg)
    m_mat = jnp.concatenate([c3, s3], axis=1).astype(bf16)
    scale = 1.0 / lseg
    mi_mat = (jnp.concatenate([c3, -s3], axis=1) * scale).transpose(0, 2, 1).astype(bf16)
    return (m_mat, mi_mat), (c, s), (nf, fc)


def _hyena_filters(lseg, w1, b1, w2, b2, freq, w3, b3, decay):
    hi = lax.Precision.HIGHEST
    t = jnp.arange(lseg, dtype=f32)
    tn = t / lseg
    bands = jnp.linspace(1e-4, HY_BANDS - 1, HY_BANDS, dtype=f32)
    ang = (2.0 * math.pi / lseg) * t[:, None] * bands[None, :]
    feat = jnp.concatenate([tn[:, None], jnp.cos(ang), -jnp.sin(ang)], axis=-1)
    hdn = jnp.sin(freq[0] * (jnp.dot(feat, w1, precision=hi) + b1))
    hdn = jnp.sin(freq[1] * (jnp.dot(hdn, w2, precision=hi) + b2))
    filt = (jnp.dot(hdn, w3, precision=hi) + b3).reshape(lseg, HY_ORDER, 2, HY_W)
    window = jnp.exp(-tn[:, None, None, None] * jnp.abs(decay)[None])
    return filt * window


def _hyena_kspec(lseg, cs, nf_fc, w1, b1, w2, b2, freq, w3, b3, decay, bias):
    hi = lax.Precision.HIGHEST
    c, s = cs
    nf, fc = nf_fc
    filt = _hyena_filters(lseg, w1, b1, w2, b2, freq, w3, b3, decay)
    out = []
    for o in range(HY_ORDER):
        hf = filt[:, o, 0].at[0].add(bias[o])
        hb = filt[:, o, 1].at[0].set(0.0)
        kr = jnp.dot(c, hf + hb, precision=hi).reshape(nf, fc, HY_W)
        ki = jnp.dot(s, hb - hf, precision=hi).reshape(nf, fc, HY_W)
        out.append(jnp.concatenate([kr, ki], axis=1))
    return out


def _lru_coef_kernel(xc_ref, w_ref, bias_ref, lam_ref, a_ref, b_ref):
    xc = xc_ref[...]
    z = jnp.dot(xc.astype(bf16), w_ref[...], preferred_element_type=f32) + bias_ref[...]
    g = jax.nn.sigmoid(z)
    for d in range(2):
        r = g[:, d * 2 * LRU_W:d * 2 * LRU_W + LRU_W]
        i = g[:, d * 2 * LRU_W + LRU_W:(d + 1) * 2 * LRU_W]
        log_a = -LRU_C * r * _softplus(-lam_ref[d:d + 1, :])
        th = jnp.tanh(log_a)
        one_minus_a2 = -2.0 * th / (1.0 - th)
        a_ref[d] = jnp.exp(log_a)
        b_ref[d] = jnp.sqrt(one_minus_a2) * i * xc


def _lru_coefs(xc2, w_all, bias_all, lam):
    r, w = xc2.shape
    tm = _pick(r, (512, 384, 256, 128))
    return pl.pallas_call(
        _lru_coef_kernel,
        out_shape=[jax.ShapeDtypeStruct((2, r, w), f32)] * 2,
        grid=(r // tm,),
        in_specs=[pl.BlockSpec((tm, w), lambda i: (i, 0)),
                  pl.BlockSpec((w, 4 * w), lambda i: (0, 0)),
                  pl.BlockSpec((1, 4 * w), lambda i: (0, 0)),
                  pl.BlockSpec((2, w), lambda i: (0, 0))],
        out_specs=[pl.BlockSpec((2, tm, w), lambda i: (0, i, 0))] * 2,
        compiler_params=_params(("parallel",), 40),
        name="lru_coefs",
    )(xc2, w_all, bias_all, lam)


def _lru_scan_kernel(af_ref, bf_ref, ab_ref, bb_ref, hf_ref, hb_ref, st_ref, *, blk):
    @pl.when(pl.program_id(1) == 0)
    def _():
        st_ref[...] = jnp.zeros_like(st_ref)

    def body(g, carry):
        hf, hb = carry
        base = pl.multiple_of(g * SUBLANES, SUBLANES)
        rbase = pl.multiple_of(blk - SUBLANES - g * SUBLANES, SUBLANES)
        for r in range(SUBLANES):
            hf = af_ref[0, 0, pl.ds(base + r, 1), :] * hf + bf_ref[0, 0, pl.ds(base + r, 1), :]
            hf_ref[0, pl.ds(base + r, 1), :] = hf
            rr = rbase + (SUBLANES - 1 - r)
            hb = ab_ref[0, 0, pl.ds(rr, 1), :] * hb + bb_ref[0, 0, pl.ds(rr, 1), :]
            hb_ref[0, pl.ds(rr, 1), :] = hb
        return hf, hb

    hf, hb = lax.fori_loop(0, blk // SUBLANES, body, (st_ref[0:1, :], st_ref[1:2, :]))
    st_ref[0:1, :] = hf
    st_ref[1:2, :] = hb


def _seg_orders(n_lat, n_ctx):
    def fwd(i):
        return jnp.where(i < n_ctx, n_lat + i, i - n_ctx)

    def bwd(i):
        return jnp.where(i < n_ctx, n_lat + n_ctx - 1 - i, n_lat - 1 - (i - n_ctx))

    return fwd, bwd


def _lru_scan(a4, b4, l_lat):
    _, nb, t_all, w = a4.shape
    lc = t_all - l_lat
    blk = _pick(math.gcd(l_lat, lc), (256, 128))
    fwd, bwd = _seg_orders(l_lat // blk, lc // blk)
    spec_f = pl.BlockSpec((1, 1, blk, w), lambda b, i: (0, b, fwd(i), 0))
    spec_b = pl.BlockSpec((1, 1, blk, w), lambda b, i: (1, b, bwd(i), 0))
    return pl.pallas_call(
        functools.partial(_lru_scan_kernel, blk=blk),
        out_shape=[jax.ShapeDtypeStruct((nb, t_all, w), f32)] * 2,
        grid=(nb, t_all // blk),
        in_specs=[spec_f, spec_f, spec_b, spec_b],
        out_specs=[pl.BlockSpec((1, blk, w), lambda b, i: (b, fwd(i), 0)),
                   pl.BlockSpec((1, blk, w), lambda b, i: (b, bwd(i), 0))],
        scratch_shapes=[pltpu.VMEM((SUBLANES, w), f32)],
        compiler_params=_params(("parallel", "arbitrary"), 32),
        name="lru_scan",
    )(a4, b4, a4, b4)


def _lru_finish_kernel(g_ref, hf_ref, hb_ref, o_ref):
    g = g_ref[...]
    gelu = 0.5 * g * (1.0 + jnp.tanh(math.sqrt(2.0 / math.pi) * (g + 0.044715 * (g * g * g))))
    o_ref[...] = (gelu * (hf_ref[...] + hb_ref[...])).astype(o_ref.dtype)


def _lru_finish(proj2, hf2, hb2):
    r, w = hf2.shape
    tm = _pick(r, (1024, 768, 512, 384, 256, 128))
    return pl.pallas_call(
        _lru_finish_kernel,
        out_shape=jax.ShapeDtypeStruct((r, w), bf16),
        grid=(r // tm,),
        in_specs=[pl.BlockSpec((tm, w), lambda i: (i, C_LRU_G // LRU_W)),
                  pl.BlockSpec((tm, w), lambda i: (i, 0)),
                  pl.BlockSpec((tm, w), lambda i: (i, 0))],
        out_specs=pl.BlockSpec((tm, w), lambda i: (i, 0)),
        compiler_params=_params(("parallel",), 32),
        name="lru_finish",
    )(proj2, hf2, hb2)


def _ssd_dir(d, x_ref, dc_ref, dr_ref, tri_ref, trit_ref, alog_l_ref, dtb_l_ref, alog_c_ref, dtb_c_ref,
             y_ref, st_ref):
    hi = lax.Precision.HIGHEST
    q = SSD_CHUNK
    p = SSD_HEAD_DIM
    ns = SSD_STATE
    per_g = SSD_HEADS // SSD_GROUPS
    xbc = x_ref[0]
    tri = tri_ref[d]
    mask = tri > 0.5
    dt_c = _softplus(dc_ref[0] + dtb_l_ref[...])
    dta_c = dt_c * (-jnp.exp(alog_l_ref[...]))
    cs_c = jnp.dot(tri, dta_c, precision=hi, preferred_element_type=f32)
    dt_r = _softplus(dr_ref[0] + dtb_c_ref[...])
    dta_r = dt_r * (-jnp.exp(alog_c_ref[...]))
    cs_r = jnp.dot(dta_r, trit_ref[d], precision=hi, preferred_element_type=f32)
    tot_r = jnp.sum(dta_r, axis=1, keepdims=True)
    ys = []
    for g in range(SSD_GROUPS):
        bm = xbc[:, SSD_W + g * ns:SSD_W + (g + 1) * ns]
        cm = xbc[:, SSD_W + SSD_GROUPS * ns + g * ns:SSD_W + SSD_GROUPS * ns + (g + 1) * ns]
        cmb = cm.astype(bf16)
        cb = lax.dot_general(cmb, bm.astype(bf16), (((1,), (1,)), ((), ())), preferred_element_type=f32)
        bm_t = bm.T
        for hg in range(per_g):
            h = g * per_g + hg
            hh = d * SSD_HEADS + h
            csc = cs_c[:, hh:hh + 1]
            csr = cs_r[hh:hh + 1, :]
            tot = tot_r[hh:hh + 1, :]
            decay = jnp.where(mask, jnp.exp(csc - csr), 0.0)
            xdt = (xbc[:, h * p:(h + 1) * p] * dt_c[:, hh:hh + 1]).astype(bf16)
            y_diag = jnp.dot((cb * decay).astype(bf16), xdt, preferred_element_type=f32)
            s_prev = st_ref[d, h]
            y_off = jnp.dot(cmb, s_prev.astype(bf16), preferred_element_type=f32) * jnp.exp(csc)
            bw_t = (bm_t * jnp.exp(tot - csr)).astype(bf16)
            st_ref[d, h] = jnp.exp(tot) * s_prev + jnp.dot(bw_t, xdt, preferred_element_type=f32)
            ys.append(y_diag + y_off)
    y_ref[0] = jnp.concatenate(ys, axis=1)


def _ssd_kernel(xf_ref, dcf_ref, drf_ref, xb_ref, dcb_ref, drb_ref, tri_ref, trit_ref,
                alog_l_ref, dtb_l_ref, alog_c_ref, dtb_c_ref, yf_ref, yb_ref, st_ref):
    @pl.when(pl.program_id(1) == 0)
    def _():
        st_ref[...] = jnp.zeros_like(st_ref)

    consts = (tri_ref, trit_ref, alog_l_ref, dtb_l_ref, alog_c_ref, dtb_c_ref)
    _ssd_dir(0, xf_ref, dcf_ref, drf_ref, *consts, yf_ref, st_ref)
    _ssd_dir(1, xb_ref, dcb_ref, drb_ref, *consts, yb_ref, st_ref)


def _ssd_scan(xbc, proj, dt_t, a_log, dt_bias, l_lat):
    nb, t_all, _ = xbc.shape
    q = SSD_CHUNK
    lc = t_all - l_lat
    fwd, bwd = _seg_orders(l_lat // q, lc // q)
    low = jnp.tril(jnp.ones((q, q), f32))
    tri = jnp.stack([low, low.T])
    trit = jnp.stack([low.T, low])
    nh2 = 2 * SSD_HEADS
    alog_l = jnp.zeros((1, LANES), f32).at[0, :nh2].set(a_log.reshape(nh2))
    dtb_l = jnp.zeros((1, LANES), f32).at[0, :nh2].set(dt_bias.reshape(nh2))

    def specs(order):
        return [pl.BlockSpec((1, q, SSD_XBC), lambda b, i: (b, order(i), 0)),
                pl.BlockSpec((1, q, LANES), lambda b, i: (b, order(i), C_DT // LANES)),
                pl.BlockSpec((1, nh2, q), lambda b, i: (b, 0, order(i)))]

    def const(shape):
        return pl.BlockSpec(shape, lambda b, i: (0,) * len(shape))

    return pl.pallas_call(
        _ssd_kernel,
        out_shape=[jax.ShapeDtypeStruct((nb, t_all, SSD_W), f32)] * 2,
        grid=(nb, t_all // q),
        in_specs=specs(fwd) + specs(bwd) + [const((2, q, q)), const((2, q, q)), const((1, LANES)),
                                            const((1, LANES)), const((nh2, 1)), const((nh2, 1))],
        out_specs=[pl.BlockSpec((1, q, SSD_W), lambda b, i: (b, fwd(i), 0)),
                   pl.BlockSpec((1, q, SSD_W), lambda b, i: (b, bwd(i), 0))],
        scratch_shapes=[pltpu.VMEM((2, SSD_HEADS, SSD_STATE, SSD_HEAD_DIM), f32)],
        compiler_params=_params(("parallel", "arbitrary"), 32),
        name="ssd_scan",
    )(xbc, proj, dt_t, xbc, proj, dt_t, tri, trit, alog_l, dtb_l,
      a_log.reshape(nh2, 1), dt_bias.reshape(nh2, 1))


def _ssd_finish_kernel(yf_ref, yb_ref, xs_ref, z_ref, dsk_ref, g_ref, o_ref):
    y = yf_ref[...] + yb_ref[...] + dsk_ref[...] * xs_ref[...]
    y = y * _silu(z_ref[...])
    y = y * lax.rsqrt(jnp.mean(y * y, axis=-1, keepdims=True) + NORM_EPS) * g_ref[...]
    o_ref[...] = y.astype(o_ref.dtype)


def _ssd_finish(yf2, yb2, xbc2, proj2, d_skip, norm_g):
    r, w = yf2.shape
    tm = _pick(r, (1024, 768, 512, 384, 256, 128))
    row = lambda i: (i, 0)
    return pl.pallas_call(
        _ssd_finish_kernel,
        out_shape=jax.ShapeDtypeStruct((r, w), bf16),
        grid=(r // tm,),
        in_specs=[pl.BlockSpec((tm, w), row), pl.BlockSpec((tm, w), row), pl.BlockSpec((tm, w), row),
                  pl.BlockSpec((tm, w), lambda i: (i, C_SSD_Z // SSD_W)),
                  pl.BlockSpec((1, w), lambda i: (0, 0)), pl.BlockSpec((1, w), lambda i: (0, 0))],
        out_specs=pl.BlockSpec((tm, w), row),
        compiler_params=_params(("parallel",), 40),
        name="ssd_finish",
    )(yf2, yb2, xbc2, proj2, jnp.repeat(d_skip, SSD_HEAD_DIM).reshape(1, w), norm_g.reshape(1, w))


def _router_kernel(f_ref, w_ref, b_ref, o_ref):
    hi = lax.Precision.HIGHEST
    logits = jnp.dot(f_ref[...].astype(f32), w_ref[...], precision=hi, preferred_element_type=f32) + b_ref[...]
    lane = lax.broadcasted_iota(jnp.int32, logits.shape, 1)
    neg = jnp.float32(-jnp.inf)
    logits = jnp.where(lane < N_EXPERTS, logits, neg)
    v1 = jnp.max(logits, axis=-1, keepdims=True)
    i1 = jnp.min(jnp.where(logits == v1, lane, GATE_LANES), axis=-1, keepdims=True)
    rest = jnp.where(lane == i1, neg, logits)
    v2 = jnp.max(rest, axis=-1, keepdims=True)
    i2 = jnp.min(jnp.where(rest == v2, lane, GATE_LANES), axis=-1, keepdims=True)
    e2 = jnp.exp(v2 - v1)
    w1 = 1.0 / (1.0 + e2)
    o_ref[...] = jnp.where(lane == i1, w1, 0.0) + jnp.where(lane == i2, e2 * w1, 0.0)


def _router(fin, w_router, b_router):
    r, d = fin.shape
    tm = _pick(r, (512, 384, 256, 128))
    wp = jnp.zeros((d, GATE_LANES), f32).at[:, :N_EXPERTS].set(w_router)
    bp = jnp.zeros((1, GATE_LANES), f32).at[0, :N_EXPERTS].set(b_router)
    return pl.pallas_call(
        _router_kernel,
        out_shape=jax.ShapeDtypeStruct((r, GATE_LANES), f32),
        grid=(r // tm,),
        in_specs=[pl.BlockSpec((tm, d), lambda i: (i, 0)),
                  pl.BlockSpec((d, GATE_LANES), lambda i: (0, 0)),
                  pl.BlockSpec((1, GATE_LANES), lambda i: (0, 0))],
        out_specs=pl.BlockSpec((tm, GATE_LANES), lambda i: (i, 0)),
        compiler_params=_params(("parallel",), 32),
        name="router",
    )(fin, wp, bp)


def _final_norm_kernel(x_ref, g_ref, o_ref):
    x = x_ref[0]
    o_ref[0] = x * lax.rsqrt(jnp.mean(x * x, axis=-1, keepdims=True) + NORM_EPS) * g_ref[...]


def _final_norm(x3, gain, l_lat):
    nb, t_all, d = x3.shape
    tr = _pick(math.gcd(l_lat, t_all), (512, 256, 128))
    return pl.pallas_call(
        _final_norm_kernel,
        out_shape=jax.ShapeDtypeStruct((nb, l_lat, d), f32),
        grid=(nb, l_lat // tr),
        in_specs=[pl.BlockSpec((1, tr, d), lambda b, j: (b, j, 0)),
                  pl.BlockSpec((1, d), lambda b, j: (0, 0))],
        out_specs=pl.BlockSpec((1, tr, d), lambda b, j: (b, j, 0)),
        compiler_params=_params(("parallel", "parallel"), 32),
        name="final_norm",
    )(x3, gain.reshape(1, d))


def _reorder_w_in(w):
    dt0 = C_SSD_XBC + SSD_XBC
    dt1 = dt0 + 2 * SSD_HEADS
    pad = jnp.zeros((w.shape[0], PROJ_W - w.shape[1]), w.dtype)
    return jnp.concatenate([w[:, :dt0], w[:, dt1:], w[:, dt0:dt1], pad], axis=1)


def _block_diag(w):
    h, bw, _ = w.shape
    eye = jnp.eye(h, dtype=w.dtype)
    return (eye[:, None, :, None] * w[:, :, None, :]).reshape(h * bw, h * bw)


def kernel(x, c, ctx, c_ctx, norm_mix, norm_ffn, ada_w, ada_b, w_in, hy_conv_w, hy_conv_b, hy_w1, hy_b1, hy_w2, hy_b2, hy_freq, hy_w3, hy_b3, hy_decay, hy_bias, lru_conv_w, lru_conv_b, lru_wa, lru_ba, lru_wx, lru_bx, lru_lambda, ssd_conv_w, ssd_conv_b, ssd_a_log, ssd_dt_bias, ssd_d, ssd_norm, att_q_norm, att_k_norm, w_branch, w_gate, b_gate, w_out, ffn_w1, ffn_w3, ffn_w2, moe_router, moe_router_b, moe_w1, moe_w3, moe_w2, norm_final):
    nb, l_lat, d = x.shape
    lc = ctx.shape[1]
    t_all = l_lat + lc
    rows = nb * t_all
    depth = norm_mix.shape[0]
    tm = _pick(t_all, (768, 384, 256, 128))

    mod_rows = -(-(nb + 1) // SUBLANES) * SUBLANES
    cvec = jnp.zeros((mod_rows, d), f32).at[:nb].set(c).at[nb].set(c_ctx)
    mods = _mods(cvec, ada_w, ada_b).reshape(depth, mod_rows, 6, d)

    cos2, sin2 = _rope_tables(l_lat, lc)
    dft_l, cs_l, nf_l = _dft_mats(l_lat)
    dft_c, cs_c, nf_c = _dft_mats(lc)

    xall = jnp.concatenate([x, ctx], axis=1).reshape(rows, d)

    for l in range(depth):
        with_ctx = l < depth - 1
        modtab = mods[l]
        h = _normmod(xall, norm_mix[l], modtab, nb, t_all, l_lat, 0, 1)
        proj2 = _mm(h, _reorder_w_in(w_in[l]).astype(bf16), f32, tm, 640)
        proj = proj2.reshape(nb, t_all, PROJ_W)

        hy_args = (hy_w1[l], hy_b1[l], hy_w2[l], hy_b2[l], hy_freq[l], hy_w3[l], hy_b3[l], hy_decay[l], hy_bias[l])
        hu = _dwconv(proj, C_HY, HY_IN, hy_conv_w[l], hy_conv_b[l], (HY_SHORT - 1) // 2, l_lat)
        k_l = _hyena_kspec(l_lat, cs_l, nf_l, *hy_args)
        z_l = _hyconv(hu, (0, 0), hu, (0, 1), l_lat, dft_l, k_l[0])
        hy_l = _hyconv(z_l, (0, 0), hu, (0, 2), l_lat, dft_l, k_l[1])
        if with_ctx:
            k_c = _hyena_kspec(lc, cs_c, nf_c, *hy_args)
            z_c = _hyconv(hu, (l_lat // lc, 0), hu, (l_lat // lc, 1), lc, dft_c, k_c[0])
            hy_c = _hyconv(z_c, (0, 0), hu, (l_lat // lc, 2), lc, dft_c, k_c[1])
        else:
            hy_c = jnp.zeros((nb, lc, HY_W), bf16)
        y_hy = jnp.concatenate([hy_l, hy_c], axis=1).reshape(rows, HY_W)

        xc = _dwconv(proj, C_LRU_X, LRU_W, lru_conv_w[l], lru_conv_b[l], LRU_CONV // 2, l_lat)
        w_all = jnp.concatenate([_block_diag(lru_wa[l, 0]), _block_diag(lru_wx[l, 0]),
                                 _block_diag(lru_wa[l, 1]), _block_diag(lru_wx[l, 1])], axis=1).astype(bf16)
        bias_all = jnp.concatenate([lru_ba[l, 0].reshape(-1), lru_bx[l, 0].reshape(-1),
                                    lru_ba[l, 1].reshape(-1), lru_bx[l, 1].reshape(-1)]).reshape(1, 4 * LRU_W)
        a2, b2 = _lru_coefs(xc.reshape(rows, LRU_W), w_all, bias_all, lru_lambda[l])
        hf, hb = _lru_scan(a2.reshape(2, nb, t_all, LRU_W), b2.reshape(2, nb, t_all, LRU_W), l_lat)
        y_lru = _lru_finish(proj2, hf.reshape(rows, LRU_W), hb.reshape(rows, LRU_W))

        xbc = _dwconv(proj, C_SSD_XBC, SSD_XBC, ssd_conv_w[l], ssd_conv_b[l], SSD_CONV // 2, l_lat, act=True)
        dt_t = proj[:, :, C_DT:C_DT + 2 * SSD_HEADS].transpose(0, 2, 1)
        yf, yb = _ssd_scan(xbc, proj, dt_t, ssd_a_log[l], ssd_dt_bias[l], l_lat)
        y_ssd = _ssd_finish(yf.reshape(rows, SSD_W), yb.reshape(rows, SSD_W), xbc.reshape(rows, SSD_XBC),
                            proj2, ssd_d[l], ssd_norm[l])

        y_att = _attention(proj, cos2, sin2, att_q_norm[l], att_k_norm[l], l_lat).reshape(rows, ATT_Q)

        wb = w_branch[l].astype(bf16)
        offs = (0, HY_W, HY_W + LRU_W, HY_W + LRU_W + SSD_W, HY_W + LRU_W + SSD_W + ATT_Q)
        wbs = [wb[offs[i]:offs[i + 1]] for i in range(4)]
        merged = _merge(h, [y_hy, y_lru, y_ssd, y_att], w_gate[l].astype(bf16), b_gate[l], wbs, tm)
        xall = _mm_resid(merged, w_out[l].astype(bf16), xall, modtab, nb, t_all, l_lat, 2, 1024)

        fin = _normmod(xall, norm_ffn[l], modtab, nb, t_all, l_lat, 3, 4)
        j = l // 2
        if l % 2 == 0:
            padc = FF_DENSE_PAD - FF_DENSE
            w1 = jnp.pad(ffn_w1[j], ((0, 0), (0, padc))).astype(bf16)
            w3 = jnp.pad(ffn_w3[j], ((0, 0), (0, padc))).astype(bf16)
            w2 = jnp.pad(ffn_w2[j], ((0, padc), (0, 0))).astype(bf16)
            hmid = _mm_swiglu(fin, w1, w3, tm, 512)
            xall = _mm_resid(hmid, w2, xall, modtab, nb, t_all, l_lat, 5, 512)
        else:
            gates = _router(fin, moe_router[j], moe_router_b[j])
            for e in range(N_EXPERTS):
                hmid = _mm_swiglu(fin, moe_w1[j, e].astype(bf16), moe_w3[j, e].astype(bf16), tm, 512)
                xall = _mm_resid(hmid, moe_w2[j, e].astype(bf16), xall, modtab, nb, t_all, l_lat, 5, 512,
                                 gates=gates, gate_col=e)

    return _final_norm(xall.reshape(nb, t_all, d), norm_final, l_lat)
```

```python
import functools
import math

import jax
import jax.numpy as jnp
from jax import lax
from jax.experimental import pallas as pl
from jax.experimental.pallas import tpu as pltpu

f32 = jnp.float32
bf16 = jnp.bfloat16

D_MODEL = 2048
DEPTH = 4
GRID_W = 64
NORM_EPS = 1e-6
HY_W = 512
HY_ORDER = 2
HY_SHORT = 3
HY_BANDS = 16
HY_IN = (HY_ORDER + 1) * HY_W
LRU_W = 512
LRU_HEADS = 8
LRU_BW = LRU_W // LRU_HEADS
LRU_CONV = 4
LRU_C = 8.0
SSD_HEADS = 8
SSD_HEAD_DIM = 64
SSD_W = SSD_HEADS * SSD_HEAD_DIM
SSD_GROUPS = 2
SSD_STATE = 128
SSD_CONV = 4
SSD_CHUNK = 128
SSD_XBC = SSD_W + 2 * SSD_GROUPS * SSD_STATE
ATT_HEADS = 8
ATT_KV_HEADS = 2
ATT_HEAD_DIM = 128
ROPE_THETA = 10000.0
ATT_Q = ATT_HEADS * ATT_HEAD_DIM
ATT_KV = ATT_KV_HEADS * ATT_HEAD_DIM
FF_DENSE = 5504
N_EXPERTS = 8
TOP_K = 2

LANES = 128
SUBLANES = 8
MIB = 2 ** 20

C_HY = 0
C_LRU_G = C_HY + HY_IN
C_LRU_X = C_LRU_G + LRU_W
C_SSD_Z = C_LRU_X + LRU_W
C_SSD_XBC = C_SSD_Z + SSD_W
C_Q = C_SSD_XBC + SSD_XBC
C_K = C_Q + ATT_Q
C_V = C_K + ATT_KV
C_DT = C_V + ATT_KV
PROJ_W = C_DT + LANES
FF_DENSE_PAD = 5632
GATE_LANES = LANES


def _params(sem, vmem_mib):
    return pltpu.CompilerParams(dimension_semantics=sem, vmem_limit_bytes=vmem_mib * MIB)


def _pick(n, cands):
    for c in cands:
        if n % c == 0:
            return c
    raise ValueError(f"no tile for {n} in {cands}")


def _silu(v):
    return v * jax.nn.sigmoid(v)


def _softplus(v):
    return jnp.maximum(v, 0.0) + jnp.log1p(jnp.exp(-jnp.abs(v)))


def _mods_kernel(c_ref, w_ref, b_ref, o_ref):
    s = _silu(c_ref[...]).astype(bf16)
    o_ref[0] = jnp.dot(s, w_ref[0].astype(bf16), preferred_element_type=f32) + b_ref[0]


def _mods(cvec, ada_w, ada_b):
    depth, d, n = ada_w.shape
    rows = cvec.shape[0]
    tn = 1024
    return pl.pallas_call(
        _mods_kernel,
        out_shape=jax.ShapeDtypeStruct((depth, rows, n), f32),
        grid=(depth, n // tn),
        in_specs=[pl.BlockSpec((rows, d), lambda l, j: (0, 0)),
                  pl.BlockSpec((1, d, tn), lambda l, j: (l, 0, j)),
                  pl.BlockSpec((1, 1, tn), lambda l, j: (l, 0, j))],
        out_specs=pl.BlockSpec((1, rows, tn), lambda l, j: (l, 0, j)),
        compiler_params=_params(("parallel", "parallel"), 40),
        name="mods",
    )(cvec, ada_w, ada_b.reshape(depth, 1, n))


def _row_mod(ml_ref, mc_ref, k, is_ctx):
    return jnp.where(is_ctx, mc_ref[0, k:k + 1, :], ml_ref[0, k:k + 1, :])


def _is_ctx_rows(tm, tpb, l_lat, axis):
    t = (pl.program_id(axis) % tpb) * tm + lax.broadcasted_iota(jnp.int32, (tm, 1), 0)
    return t >= l_lat


def _normmod_kernel(x_ref, g_ref, ml_ref, mc_ref, o_ref, *, tm, tpb, l_lat, k_shift, k_scale):
    x = x_ref[...]
    y = x * lax.rsqrt(jnp.mean(x * x, axis=-1, keepdims=True) + NORM_EPS) * g_ref[...]
    is_ctx = _is_ctx_rows(tm, tpb, l_lat, 0)
    shift = _row_mod(ml_ref, mc_ref, k_shift, is_ctx)
    scale = _row_mod(ml_ref, mc_ref, k_scale, is_ctx)
    o_ref[...] = (y * (1.0 + scale) + shift).astype(o_ref.dtype)


def _normmod(x2, gain, modtab, nb, t_all, l_lat, k_shift, k_scale, out_dtype=bf16):
    r, d = x2.shape
    tm = _pick(t_all, (768, 384, 256, 128))
    tpb = t_all // tm
    return pl.pallas_call(
        functools.partial(_normmod_kernel, tm=tm, tpb=tpb, l_lat=l_lat, k_shift=k_shift, k_scale=k_scale),
        out_shape=jax.ShapeDtypeStruct((r, d), out_dtype),
        grid=(r // tm,),
        in_specs=[pl.BlockSpec((tm, d), lambda i: (i, 0)),
                  pl.BlockSpec((1, d), lambda i: (0, 0)),
                  pl.BlockSpec((1, 6, d), lambda i: (i // tpb, 0, 0)),
                  pl.BlockSpec((1, 6, d), lambda i: (nb, 0, 0))],
        out_specs=pl.BlockSpec((tm, d), lambda i: (i, 0)),
        compiler_params=_params(("parallel",), 56),
        name="normmod",
    )(x2, gain.reshape(1, d), modtab, modtab)


def _mm_kernel(a_ref, w_ref, o_ref):
    o_ref[...] = jnp.dot(a_ref[...], w_ref[...], preferred_element_type=f32).astype(o_ref.dtype)


def _mm(a, w, out_dtype, tm, tn, vmem=48):
    m, k = a.shape
    n = w.shape[1]
    return pl.pallas_call(
        _mm_kernel,
        out_shape=jax.ShapeDtypeStruct((m, n), out_dtype),
        grid=(n // tn, m // tm),
        in_specs=[pl.BlockSpec((tm, k), lambda j, i: (i, 0)),
                  pl.BlockSpec((k, tn), lambda j, i: (0, j))],
        out_specs=pl.BlockSpec((tm, tn), lambda j, i: (i, j)),
        compiler_params=_params(("parallel", "parallel"), vmem),
        name="mm",
    )(a, w)


def _mm_swiglu_kernel(a_ref, w1_ref, w3_ref, o_ref):
    a = a_ref[...]
    g = jnp.dot(a, w1_ref[...], preferred_element_type=f32)
    u = jnp.dot(a, w3_ref[...], preferred_element_type=f32)
    o_ref[...] = (_silu(g) * u).astype(o_ref.dtype)


def _mm_swiglu(a, w1, w3, tm, tn, vmem=48):
    m, k = a.shape
    n = w1.shape[1]
    wspec = pl.BlockSpec((k, tn), lambda j, i: (0, j))
    return pl.pallas_call(
        _mm_swiglu_kernel,
        out_shape=jax.ShapeDtypeStruct((m, n), bf16),
        grid=(n // tn, m // tm),
        in_specs=[pl.BlockSpec((tm, k), lambda j, i: (i, 0)), wspec, wspec],
        out_specs=pl.BlockSpec((tm, tn), lambda j, i: (i, j)),
        compiler_params=_params(("parallel", "parallel"), vmem),
        name="mm_swiglu",
    )(a, w1, w3)


def _mm_resid_kernel(a_ref, w_ref, x_ref, ml_ref, mc_ref, o_ref, *, tm, tpb, l_lat, k_mod):
    acc = jnp.dot(a_ref[...], w_ref[...], preferred_element_type=f32)
    mod = _row_mod(ml_ref, mc_ref, k_mod, _is_ctx_rows(tm, tpb, l_lat, 1))
    o_ref[...] = x_ref[...] + mod * acc


def _mm_resid(a, w, x2, modtab, nb, t_all, l_lat, k_mod, tn, vmem=48):
    m, k = a.shape
    n = w.shape[1]
    tm = _pick(t_all, (768, 384, 256, 128))
    tpb = t_all // tm
    return pl.pallas_call(
        functools.partial(_mm_resid_kernel, tm=tm, tpb=tpb, l_lat=l_lat, k_mod=k_mod),
        out_shape=jax.ShapeDtypeStruct((m, n), f32),
        grid=(n // tn, m // tm),
        in_specs=[pl.BlockSpec((tm, k), lambda j, i: (i, 0)),
                  pl.BlockSpec((k, tn), lambda j, i: (0, j)),
                  pl.BlockSpec((tm, tn), lambda j, i: (i, j)),
                  pl.BlockSpec((1, 6, tn), lambda j, i: (i // tpb, 0, j)),
                  pl.BlockSpec((1, 6, tn), lambda j, i: (nb, 0, j))],
        out_specs=pl.BlockSpec((tm, tn), lambda j, i: (i, j)),
        input_output_aliases={2: 0},
        compiler_params=_params(("parallel", "parallel"), vmem),
        name="mm_resid",
    )(a, w, x2, modtab, modtab)


def _merge_kernel(h_ref, y0_ref, y1_ref, y2_ref, y3_ref, wg_ref, bg_ref,
                  wb0_ref, wb1_ref, wb2_ref, wb3_ref, o_ref):
    h = h_ref[...]
    acc = None
    for i, (y_ref, wb_ref) in enumerate(((y0_ref, wb0_ref), (y1_ref, wb1_ref),
                                         (y2_ref, wb2_ref), (y3_ref, wb3_ref))):
        g = jnp.dot(h, wg_ref[i], preferred_element_type=f32) + bg_ref[i]
        p = jnp.dot(y_ref[...], wb_ref[...], preferred_element_type=f32)
        term = jax.nn.sigmoid(g) * p
        acc = term if acc is None else acc + term
    o_ref[...] = acc.astype(o_ref.dtype)


def _merge(h, ys, wg, bg, wbs, tm, tn=256):
    m, d = h.shape
    n = wg.shape[2]
    in_specs = [pl.BlockSpec((tm, d), lambda j, i: (i, 0))]
    in_specs += [pl.BlockSpec((tm, y.shape[1]), lambda j, i: (i, 0)) for y in ys]
    in_specs += [pl.BlockSpec((4, d, tn), lambda j, i: (0, 0, j)),
                 pl.BlockSpec((4, 1, tn), lambda j, i: (0, 0, j))]
    in_specs += [pl.BlockSpec((wb.shape[0], tn), lambda j, i: (0, j)) for wb in wbs]
    return pl.pallas_call(
        _merge_kernel,
        out_shape=jax.ShapeDtypeStruct((m, n), bf16),
        grid=(n // tn, m // tm),
        in_specs=in_specs,
        out_specs=pl.BlockSpec((tm, tn), lambda j, i: (i, j)),
        compiler_params=_params(("parallel", "parallel"), 48),
        name="merge",
    )(h, *ys, wg, bg.reshape(4, 1, n), *wbs)


def _norm_rope(x, gain, cos, sin):
    y = x * lax.rsqrt(jnp.mean(x * x, axis=-1, keepdims=True) + NORM_EPS) * gain
    lane = lax.broadcasted_iota(jnp.int32, y.shape, 1)
    partner = jnp.where(lane % 2 == 0, pltpu.roll(y, ATT_HEAD_DIM - 1, 1), pltpu.roll(y, 1, 1))
    return y * cos + partner * sin


def _attn_kernel(q_ref, k_ref, v_ref, cos_ref, sin_ref, qg_ref, kg_ref, o_ref, ks_ref, vs_ref,
                 *, tq, l_lat, n_lat_blocks):
    qi = pl.program_id(2)
    grp = ATT_HEADS // ATT_KV_HEADS
    hd = ATT_HEAD_DIM

    @pl.when(qi == 0)
    def _():
        ks_ref[...] = _norm_rope(k_ref[0], kg_ref[...], cos_ref[...], sin_ref[...]).astype(bf16)
        vs_ref[:, :hd] = v_ref[0].astype(bf16)
        vs_ref[:, hd:] = jnp.ones((vs_ref.shape[0], hd), bf16)

    r0 = pl.multiple_of(qi * tq, tq)
    cos = cos_ref[pl.ds(r0, tq), :]
    sin = sin_ref[pl.ds(r0, tq), :]
    q = q_ref[0]
    sc = (hd ** -0.5) * math.log2(math.e)
    qhs = [(_norm_rope(q[:, h * hd:(h + 1) * hd], qg_ref[...], cos, sin) * sc).astype(bf16) for h in range(grp)]

    def attend(ks, vs):
        for h in range(grp):
            s = lax.dot_general(qhs[h], ks, (((1,), (1,)), ((), ())), preferred_element_type=f32)
            p = jnp.exp2(s - jnp.max(s, axis=-1, keepdims=True)).astype(bf16)
            oe = jnp.dot(p, vs, preferred_element_type=f32)
            o_ref[0, :, h * hd:(h + 1) * hd] = (oe[:, :hd] / oe[:, hd:hd + 1]).astype(o_ref.dtype)

    @pl.when(qi < n_lat_blocks)
    def _():
        attend(ks_ref[...], vs_ref[...])

    @pl.when(qi >= n_lat_blocks)
    def _():
        attend(ks_ref[l_lat:, :], vs_ref[l_lat:, :])


def _attention(proj, cos2, sin2, q_gain, k_gain, l_lat):
    nb, t_all, _ = proj.shape
    lc = t_all - l_lat
    tq = lc if (l_lat % lc == 0 and lc <= 256) else _pick(math.gcd(l_lat, lc), (256, 128))
    grp = ATT_HEADS // ATT_KV_HEADS
    qw = grp * ATT_HEAD_DIM
    hd = ATT_HEAD_DIM
    return pl.pallas_call(
        functools.partial(_attn_kernel, tq=tq, l_lat=l_lat, n_lat_blocks=l_lat // tq),
        out_shape=jax.ShapeDtypeStruct((nb, t_all, ATT_Q), bf16),
        grid=(nb, ATT_KV_HEADS, t_all // tq),
        in_specs=[pl.BlockSpec((1, tq, qw), lambda b, g, i: (b, i, C_Q // qw + g)),
                  pl.BlockSpec((1, t_all, hd), lambda b, g, i: (b, 0, C_K // hd + g)),
                  pl.BlockSpec((1, t_all, hd), lambda b, g, i: (b, 0, C_V // hd + g)),
                  pl.BlockSpec((t_all, hd), lambda b, g, i: (0, 0)),
                  pl.BlockSpec((t_all, hd), lambda b, g, i: (0, 0)),
                  pl.BlockSpec((1, hd), lambda b, g, i: (0, 0)),
                  pl.BlockSpec((1, hd), lambda b, g, i: (0, 0))],
        out_specs=pl.BlockSpec((1, tq, qw), lambda b, g, i: (b, i, g)),
        scratch_shapes=[pltpu.VMEM((t_all, hd), bf16), pltpu.VMEM((t_all, 2 * hd), bf16)],
        compiler_params=_params(("parallel", "parallel", "arbitrary"), 56),
        name="attention",
    )(proj, proj, proj, cos2, sin2, q_gain.reshape(1, hd), k_gain.reshape(1, hd))


def _rope_tables(l_lat, lc):
    axis_dim = ATT_HEAD_DIM // 2
    rows = l_lat // GRID_W
    row_id = jnp.repeat(jnp.arange(rows, dtype=f32), GRID_W)
    col_id = jnp.tile(jnp.arange(GRID_W, dtype=f32), rows)
    inv = ROPE_THETA ** (-jnp.arange(0, axis_dim, 2, dtype=f32) / axis_dim)
    ang = jnp.concatenate([row_id[:, None] * inv, col_id[:, None] * inv], axis=-1)
    cos = jnp.repeat(jnp.cos(ang), 2, axis=-1)
    sin = jnp.repeat(jnp.sin(ang), 2, axis=-1)
    sign = jnp.tile(jnp.array([-1.0, 1.0], f32), ATT_HEAD_DIM // 2)
    cos2 = jnp.concatenate([cos, jnp.ones((lc, ATT_HEAD_DIM), f32)], axis=0)
    sin2 = jnp.concatenate([sin * sign, jnp.zeros((lc, ATT_HEAD_DIM), f32)], axis=0)
    return cos2, sin2


def _dwconv_kernel(u_ref, w_ref, b_ref, o_ref, *, ksz, pad_left, l_lat, t_all, act):
    u = u_ref[0]
    t = lax.broadcasted_iota(jnp.int32, (t_all, 1), 0)
    is_ctx = t >= l_lat
    t_loc = jnp.where(is_ctx, t - l_lat, t)
    seg_len = jnp.where(is_ctx, t_all - l_lat, l_lat)
    acc = None
    for k in range(ksz):
        s = k - pad_left
        if s == 0:
            term = u
        else:
            shifted = pltpu.roll(u, (t_all - s) % t_all, 0)
            term = jnp.where((t_loc + s >= 0) & (t_loc + s < seg_len), shifted, 0.0)
        term = term * w_ref[k:k + 1, :]
        acc = term if acc is None else acc + term
    acc = acc + b_ref[...]
    if act:
        acc = _silu(acc)
    o_ref[0] = acc.astype(o_ref.dtype)


def _dwconv(proj, col0, width, w, b, pad_left, l_lat, act=False):
    nb, t_all, _ = proj.shape
    ksz = w.shape[0]
    cw = 256
    return pl.pallas_call(
        functools.partial(_dwconv_kernel, ksz=ksz, pad_left=pad_left, l_lat=l_lat, t_all=t_all, act=act),
        out_shape=jax.ShapeDtypeStruct((nb, t_all, width), f32),
        grid=(nb, width // cw),
        in_specs=[pl.BlockSpec((1, t_all, cw), lambda bi, c: (bi, 0, col0 // cw + c)),
                  pl.BlockSpec((ksz, cw), lambda bi, c: (0, c)),
                  pl.BlockSpec((1, cw), lambda bi, c: (0, c))],
        out_specs=pl.BlockSpec((1, t_all, cw), lambda bi, c: (bi, 0, c)),
        compiler_params=_params(("parallel", "parallel"), 48),
        name="dwconv",
    )(proj, w, b.reshape(1, width))


def _hyconv_kernel(u_ref, g_ref, m_ref, mi_ref, k_ref, o_ref, ub_ref, acc_ref, *, fc, nf):
    f = pl.program_id(2)

    @pl.when(f == 0)
    def _():
        ub_ref[...] = u_ref[0].astype(bf16)
        acc_ref[...] = jnp.zeros_like(acc_ref)

    spec = jnp.dot(m_ref[0], ub_ref[...], preferred_element_type=f32)
    a, bv = spec[:fc], spec[fc:]
    kk = k_ref[0]
    kr, ki = kk[:fc], kk[fc:]
    y = jnp.concatenate([a * kr + bv * ki, a * ki - bv * kr], axis=0).astype(bf16)
    acc_ref[...] += jnp.dot(mi_ref[0], y, preferred_element_type=f32)

    @pl.when(f == nf - 1)
    def _():
        o_ref[0] = (acc_ref[...] * g_ref[0]).astype(o_ref.dtype)


def _hyconv(u, u_blk, gate, gate_blk, lseg, dft, kspec):
    m_mat, mi_mat = dft
    nf, fc2, _ = m_mat.shape
    fc = fc2 // 2
    nb = u.shape[0]
    cw = HY_W
    return pl.pallas_call(
        functools.partial(_hyconv_kernel, fc=fc, nf=nf),
        out_shape=jax.ShapeDtypeStruct((nb, lseg, cw), bf16),
        grid=(nb, 1, nf),
        in_specs=[pl.BlockSpec((1, lseg, cw), lambda b, c, f: (b,) + u_blk),
                  pl.BlockSpec((1, lseg, cw), lambda b, c, f: (b,) + gate_blk),
                  pl.BlockSpec((1, fc2, lseg), lambda b, c, f: (f, 0, 0)),
                  pl.BlockSpec((1, lseg, fc2), lambda b, c, f: (f, 0, 0)),
                  pl.BlockSpec((1, fc2, cw), lambda b, c, f: (f, 0, 0))],
        out_specs=pl.BlockSpec((1, lseg, cw), lambda b, c, f: (b, 0, 0)),
        scratch_shapes=[pltpu.VMEM((lseg, cw), bf16), pltpu.VMEM((lseg, cw), f32)],
        compiler_params=_params(("parallel", "arbitrary", "arbitrary"), 52),
        name="hyconv",
    )(u, gate, m_mat, mi_mat, kspec)


def _dft_cs(lseg):
    n = 2 * lseg
    f = jnp.arange(lseg, dtype=jnp.int32)[:, None]
    s = jnp.arange(lseg, dtype=jnp.int32)[None, :]
    ph = ((2 * f + 1) * s) % (2 * n)
    ang = ph.astype(f32) * (math.pi / n)
    return jnp.cos(ang), jnp.sin(ang)


def _dft_mats(lseg):
    fc = min(256, lseg // 2)
    nf = lseg // fc
    c, s = _dft_cs(lseg)
    c3 = c.reshape(nf, fc, lseg)
    s3 = s.reshape(nf, fc, lseg)
    m_mat = jnp.concatenate([c3, s3], axis=1).astype(bf16)
    scale = 1.0 / lseg
    mi_mat = (jnp.concatenate([c3, -s3], axis=1) * scale).transpose(0, 2, 1).astype(bf16)
    return (m_mat, mi_mat), (c, s), (nf, fc)


def _hyena_filters(lseg, w1, b1, w2, b2, freq, w3, b3, decay):
    hi = lax.Precision.HIGHEST
    t = jnp.arange(lseg, dtype=f32)
    tn = t / lseg
    bands = jnp.linspace(1e-4, HY_BANDS - 1, HY_BANDS, dtype=f32)
    ang = (2.0 * math.pi / lseg) * t[:, None] * bands[None, :]
    feat = jnp.concatenate([tn[:, None], jnp.cos(ang), -jnp.sin(ang)], axis=-1)
    hdn = jnp.sin(freq[0] * (jnp.dot(feat, w1, precision=hi) + b1))
    hdn = jnp.sin(freq[1] * (jnp.dot(hdn, w2, precision=hi) + b2))
    filt = (jnp.dot(hdn, w3, precision=hi) + b3).reshape(lseg, HY_ORDER, 2, HY_W)
    window = jnp.exp(-tn[:, None, None, None] * jnp.abs(decay)[None])
    return filt * window


def _hyena_kspec(lseg, cs, nf_fc, w1, b1, w2, b2, freq, w3, b3, decay, bias):
    hi = lax.Precision.HIGHEST
    c, s = cs
    nf, fc = nf_fc
    filt = _hyena_filters(lseg, w1, b1, w2, b2, freq, w3, b3, decay)
    out = []
    for o in range(HY_ORDER):
        hf = filt[:, o, 0].at[0].add(bias[o])
        hb = filt[:, o, 1].at[0].set(0.0)
        kr = jnp.dot(c, hf + hb, precision=hi).reshape(nf, fc, HY_W)
        ki = jnp.dot(s, hb - hf, precision=hi).reshape(nf, fc, HY_W)
        out.append(jnp.concatenate([kr, ki], axis=1))
    return out


def _lru_coef_kernel(xc_ref, w_ref, bias_ref, lam_ref, a_ref, b_ref):
    xc = xc_ref[...]
    z = jnp.dot(xc.astype(bf16), w_ref[...], preferred_element_type=f32) + bias_ref[...]
    g = jax.nn.sigmoid(z)
    for d in range(2):
        r = g[:, d * 2 * LRU_W:d * 2 * LRU_W + LRU_W]
        i = g[:, d * 2 * LRU_W + LRU_W:(d + 1) * 2 * LRU_W]
        log_a = -LRU_C * r * _softplus(-lam_ref[d:d + 1, :])
        th = jnp.tanh(log_a)
        one_minus_a2 = -2.0 * th / (1.0 - th)
        a_ref[d] = jnp.exp(log_a)
        b_ref[d] = jnp.sqrt(one_minus_a2) * i * xc


def _lru_coefs(xc2, w_all, bias_all, lam):
    r, w = xc2.shape
    tm = _pick(r, (512, 384, 256, 128))
    return pl.pallas_call(
        _lru_coef_kernel,
        out_shape=[jax.ShapeDtypeStruct((2, r, w), f32)] * 2,
        grid=(r // tm,),
        in_specs=[pl.BlockSpec((tm, w), lambda i: (i, 0)),
                  pl.BlockSpec((w, 4 * w), lambda i: (0, 0)),
                  pl.BlockSpec((1, 4 * w), lambda i: (0, 0)),
                  pl.BlockSpec((2, w), lambda i: (0, 0))],
        out_specs=[pl.BlockSpec((2, tm, w), lambda i: (0, i, 0))] * 2,
        compiler_params=_params(("parallel",), 40),
        name="lru_coefs",
    )(xc2, w_all, bias_all, lam)


def _lru_scan_kernel(af_ref, bf_ref, ab_ref, bb_ref, hf_ref, hb_ref, st_ref, *, blk):
    @pl.when(pl.program_id(1) == 0)
    def _():
        st_ref[...] = jnp.zeros_like(st_ref)

    def body(g, carry):
        hf, hb = carry
        base = pl.multiple_of(g * SUBLANES, SUBLANES)
        rbase = pl.multiple_of(blk - SUBLANES - g * SUBLANES, SUBLANES)
        for r in range(SUBLANES):
            hf = af_ref[0, 0, pl.ds(base + r, 1), :] * hf + bf_ref[0, 0, pl.ds(base + r, 1), :]
            hf_ref[0, pl.ds(base + r, 1), :] = hf
            rr = rbase + (SUBLANES - 1 - r)
            hb = ab_ref[0, 0, pl.ds(rr, 1), :] * hb + bb_ref[0, 0, pl.ds(rr, 1), :]
            hb_ref[0, pl.ds(rr, 1), :] = hb
        return hf, hb

    hf, hb = lax.fori_loop(0, blk // SUBLANES, body, (st_ref[0:1, :], st_ref[1:2, :]))
    st_ref[0:1, :] = hf
    st_ref[1:2, :] = hb


def _seg_orders(n_lat, n_ctx):
    def fwd(i):
        return jnp.where(i < n_ctx, n_lat + i, i - n_ctx)

    def bwd(i):
        return jnp.where(i < n_ctx, n_lat + n_ctx - 1 - i, n_lat - 1 - (i - n_ctx))

    return fwd, bwd


def _lru_scan(a4, b4, l_lat):
    _, nb, t_all, w = a4.shape
    lc = t_all - l_lat
    blk = _pick(math.gcd(l_lat, lc), (256, 128))
    fwd, bwd = _seg_orders(l_lat // blk, lc // blk)
    spec_f = pl.BlockSpec((1, 1, blk, w), lambda b, i: (0, b, fwd(i), 0))
    spec_b = pl.BlockSpec((1, 1, blk, w), lambda b, i: (1, b, bwd(i), 0))
    return pl.pallas_call(
        functools.partial(_lru_scan_kernel, blk=blk),
        out_shape=[jax.ShapeDtypeStruct((nb, t_all, w), f32)] * 2,
        grid=(nb, t_all // blk),
        in_specs=[spec_f, spec_f, spec_b, spec_b],
        out_specs=[pl.BlockSpec((1, blk, w), lambda b, i: (b, fwd(i), 0)),
                   pl.BlockSpec((1, blk, w), lambda b, i: (b, bwd(i), 0))],
        scratch_shapes=[pltpu.VMEM((SUBLANES, w), f32)],
        compiler_params=_params(("parallel", "arbitrary"), 32),
        name="lru_scan",
    )(a4, b4, a4, b4)


def _lru_finish_kernel(g_ref, hf_ref, hb_ref, o_ref):
    g = g_ref[...]
    gelu = 0.5 * g * (1.0 + jnp.tanh(math.sqrt(2.0 / math.pi) * (g + 0.044715 * (g * g * g))))
    o_ref[...] = (gelu * (hf_ref[...] + hb_ref[...])).astype(o_ref.dtype)


def _lru_finish(proj2, hf2, hb2):
    r, w = hf2.shape
    tm = _pick(r, (1024, 768, 512, 384, 256, 128))
    return pl.pallas_call(
        _lru_finish_kernel,
        out_shape=jax.ShapeDtypeStruct((r, w), bf16),
        grid=(r // tm,),
        in_specs=[pl.BlockSpec((tm, w), lambda i: (i, C_LRU_G // LRU_W)),
                  pl.BlockSpec((tm, w), lambda i: (i, 0)),
                  pl.BlockSpec((tm, w), lambda i: (i, 0))],
        out_specs=pl.BlockSpec((tm, w), lambda i: (i, 0)),
        compiler_params=_params(("parallel",), 32),
        name="lru_finish",
    )(proj2, hf2, hb2)


def _ssd_dir(d, x_ref, dc_ref, dr_ref, tri_ref, trit_ref, alog_l_ref, dtb_l_ref, alog_c_ref, dtb_c_ref,
             y_ref, st_ref):
    hi = lax.Precision.HIGHEST
    q = SSD_CHUNK
    p = SSD_HEAD_DIM
    ns = SSD_STATE
    per_g = SSD_HEADS // SSD_GROUPS
    xbc = x_ref[0]
    tri = tri_ref[d]
    mask = tri > 0.5
    dt_c = _softplus(dc_ref[0] + dtb_l_ref[...])
    dta_c = dt_c * (-jnp.exp(alog_l_ref[...]))
    cs_c = jnp.dot(tri, dta_c, precision=hi, preferred_element_type=f32)
    dt_r = _softplus(dr_ref[0] + dtb_c_ref[...])
    dta_r = dt_r * (-jnp.exp(alog_c_ref[...]))
    cs_r = jnp.dot(dta_r, trit_ref[d], precision=hi, preferred_element_type=f32)
    tot_r = jnp.sum(dta_r, axis=1, keepdims=True)
    ys = []
    for g in range(SSD_GROUPS):
        bm = xbc[:, SSD_W + g * ns:SSD_W + (g + 1) * ns]
        cm = xbc[:, SSD_W + SSD_GROUPS * ns + g * ns:SSD_W + SSD_GROUPS * ns + (g + 1) * ns]
        cmb = cm.astype(bf16)
        cb = lax.dot_general(cmb, bm.astype(bf16), (((1,), (1,)), ((), ())), preferred_element_type=f32)
        bm_t = bm.T
        for hg in range(per_g):
            h = g * per_g + hg
            hh = d * SSD_HEADS + h
            csc = cs_c[:, hh:hh + 1]
            csr = cs_r[hh:hh + 1, :]
            tot = tot_r[hh:hh + 1, :]
            decay = jnp.where(mask, jnp.exp(csc - csr), 0.0)
            xdt = (xbc[:, h * p:(h + 1) * p] * dt_c[:, hh:hh + 1]).astype(bf16)
            y_diag = jnp.dot((cb * decay).astype(bf16), xdt, preferred_element_type=f32)
            s_prev = st_ref[d, h]
            y_off = jnp.dot(cmb, s_prev.astype(bf16), preferred_element_type=f32) * jnp.exp(csc)
            bw_t = (bm_t * jnp.exp(tot - csr)).astype(bf16)
            st_ref[d, h] = jnp.exp(tot) * s_prev + jnp.dot(bw_t, xdt, preferred_element_type=f32)
            ys.append(y_diag + y_off)
    y_ref[0] = jnp.concatenate(ys, axis=1)


def _ssd_kernel(xf_ref, dcf_ref, drf_ref, xb_ref, dcb_ref, drb_ref, tri_ref, trit_ref,
                alog_l_ref, dtb_l_ref, alog_c_ref, dtb_c_ref, yf_ref, yb_ref, st_ref):
    @pl.when(pl.program_id(1) == 0)
    def _():
        st_ref[...] = jnp.zeros_like(st_ref)

    consts = (tri_ref, trit_ref, alog_l_ref, dtb_l_ref, alog_c_ref, dtb_c_ref)
    _ssd_dir(0, xf_ref, dcf_ref, drf_ref, *consts, yf_ref, st_ref)
    _ssd_dir(1, xb_ref, dcb_ref, drb_ref, *consts, yb_ref, st_ref)


def _ssd_scan(xbc, proj, dt_t, a_log, dt_bias, l_lat):
    nb, t_all, _ = xbc.shape
    q = SSD_CHUNK
    lc = t_all - l_lat
    fwd, bwd = _seg_orders(l_lat // q, lc // q)
    low = jnp.tril(jnp.ones((q, q), f32))
    tri = jnp.stack([low, low.T])
    trit = jnp.stack([low.T, low])
    nh2 = 2 * SSD_HEADS
    alog_l = jnp.zeros((1, LANES), f32).at[0, :nh2].set(a_log.reshape(nh2))
    dtb_l = jnp.zeros((1, LANES), f32).at[0, :nh2].set(dt_bias.reshape(nh2))

    def specs(order):
        return [pl.BlockSpec((1, q, SSD_XBC), lambda b, i: (b, order(i), 0)),
                pl.BlockSpec((1, q, LANES), lambda b, i: (b, order(i), C_DT // LANES)),
                pl.BlockSpec((1, nh2, q), lambda b, i: (b, 0, order(i)))]

    def const(shape):
        return pl.BlockSpec(shape, lambda b, i: (0,) * len(shape))

    return pl.pallas_call(
        _ssd_kernel,
        out_shape=[jax.ShapeDtypeStruct((nb, t_all, SSD_W), f32)] * 2,
        grid=(nb, t_all // q),
        in_specs=specs(fwd) + specs(bwd) + [const((2, q, q)), const((2, q, q)), const((1, LANES)),
                                            const((1, LANES)), const((nh2, 1)), const((nh2, 1))],
        out_specs=[pl.BlockSpec((1, q, SSD_W), lambda b, i: (b, fwd(i), 0)),
                   pl.BlockSpec((1, q, SSD_W), lambda b, i: (b, bwd(i), 0))],
        scratch_shapes=[pltpu.VMEM((2, SSD_HEADS, SSD_STATE, SSD_HEAD_DIM), f32)],
        compiler_params=_params(("parallel", "arbitrary"), 32),
        name="ssd_scan",
    )(xbc, proj, dt_t, xbc, proj, dt_t, tri, trit, alog_l, dtb_l,
      a_log.reshape(nh2, 1), dt_bias.reshape(nh2, 1))


def _ssd_finish_kernel(yf_ref, yb_ref, xs_ref, z_ref, dsk_ref, g_ref, o_ref):
    y = yf_ref[...] + yb_ref[...] + dsk_ref[...] * xs_ref[...]
    y = y * _silu(z_ref[...])
    y = y * lax.rsqrt(jnp.mean(y * y, axis=-1, keepdims=True) + NORM_EPS) * g_ref[...]
    o_ref[...] = y.astype(o_ref.dtype)


def _ssd_finish(yf2, yb2, xbc2, proj2, d_skip, norm_g):
    r, w = yf2.shape
    tm = _pick(r, (1024, 768, 512, 384, 256, 128))
    row = lambda i: (i, 0)
    return pl.pallas_call(
        _ssd_finish_kernel,
        out_shape=jax.ShapeDtypeStruct((r, w), bf16),
        grid=(r // tm,),
        in_specs=[pl.BlockSpec((tm, w), row), pl.BlockSpec((tm, w), row), pl.BlockSpec((tm, w), row),
                  pl.BlockSpec((tm, w), lambda i: (i, C_SSD_Z // SSD_W)),
                  pl.BlockSpec((1, w), lambda i: (0, 0)), pl.BlockSpec((1, w), lambda i: (0, 0))],
        out_specs=pl.BlockSpec((tm, w), row),
        compiler_params=_params(("parallel",), 40),
        name="ssd_finish",
    )(yf2, yb2, xbc2, proj2, jnp.repeat(d_skip, SSD_HEAD_DIM).reshape(1, w), norm_g.reshape(1, w))


def _router_kernel(f_ref, w_ref, b_ref, idx_ref, wt_ref):
    hi = lax.Precision.HIGHEST
    logits = jnp.dot(f_ref[...], w_ref[...], precision=hi, preferred_element_type=f32) + b_ref[...]
    lane = lax.broadcasted_iota(jnp.int32, logits.shape, 1)
    neg = jnp.float32(-jnp.inf)
    logits = jnp.where(lane < N_EXPERTS, logits, neg)
    v1 = jnp.max(logits, axis=-1, keepdims=True)
    i1 = jnp.min(jnp.where(logits == v1, lane, GATE_LANES), axis=-1, keepdims=True)
    rest = jnp.where(lane == i1, neg, logits)
    v2 = jnp.max(rest, axis=-1, keepdims=True)
    i2 = jnp.min(jnp.where(rest == v2, lane, GATE_LANES), axis=-1, keepdims=True)
    e2 = jnp.exp(v2 - v1)
    w1 = 1.0 / (1.0 + e2)
    idx_ref[...] = jnp.where(lane == 0, i1, jnp.where(lane == 1, i2, 0))
    wt_ref[...] = jnp.where(lane == 0, w1, jnp.where(lane == 1, e2 * w1, 0.0))


def _router(fin, w_router, b_router):
    r, d = fin.shape
    tm = _pick(r, (512, 384, 256, 128))
    wp = jnp.zeros((d, GATE_LANES), f32).at[:, :N_EXPERTS].set(w_router)
    bp = jnp.zeros((1, GATE_LANES), f32).at[0, :N_EXPERTS].set(b_router)
    idx, wts = pl.pallas_call(
        _router_kernel,
        out_shape=[jax.ShapeDtypeStruct((r, GATE_LANES), jnp.int32), jax.ShapeDtypeStruct((r, GATE_LANES), f32)],
        grid=(r // tm,),
        in_specs=[pl.BlockSpec((tm, d), lambda i: (i, 0)),
                  pl.BlockSpec((d, GATE_LANES), lambda i: (0, 0)),
                  pl.BlockSpec((1, GATE_LANES), lambda i: (0, 0))],
        out_specs=[pl.BlockSpec((tm, GATE_LANES), lambda i: (i, 0))] * 2,
        compiler_params=_params(("parallel",), 40),
        name="router",
    )(fin, wp, bp)
    return idx[:, :TOP_K], wts[:, :TOP_K]


MOE_TG = 512


def _route_meta(e_idx, wts, rg):
    n_assign = e_idx.size
    e_flat = e_idx.reshape(n_assign)
    onehot = (e_flat[:, None] == jnp.arange(N_EXPERTS, dtype=jnp.int32)[None, :]).astype(jnp.int32)
    csum = jnp.cumsum(onehot, axis=0)
    counts = csum[-1]
    rank = jnp.take_along_axis(csum, e_flat[:, None], axis=1)[:, 0] - 1
    gsize = ((counts + MOE_TG - 1) // MOE_TG) * MOE_TG
    ends = jnp.cumsum(gsize)
    pos = (ends - gsize)[e_flat] + rank
    tile_start = jnp.arange(rg // MOE_TG, dtype=jnp.int32) * MOE_TG
    tile_expert = jnp.minimum(jnp.sum((tile_start[:, None] >= ends[None, :]).astype(jnp.int32), axis=1),
                              N_EXPERTS - 1).astype(jnp.int32)
    n_used = (ends[-1:] // MOE_TG).astype(jnp.int32)
    src = jnp.zeros((rg,), jnp.int32).at[pos].set(jnp.arange(n_assign, dtype=jnp.int32) // TOP_K)
    rowgate = jnp.zeros((rg,), f32).at[pos].set(wts.reshape(n_assign))
    return pos.reshape(e_idx.shape).astype(jnp.int32), src, rowgate.reshape(rg, 1), tile_expert, n_used


def _row_gather_kernel(src_ref, x_hbm, o_ref, buf_ref, sem):
    rows = buf_ref.shape[0]

    def issue(r, carry):
        pltpu.make_async_copy(x_hbm.at[pl.ds(src_ref[r], 1), :], buf_ref.at[pl.ds(r, 1), :], sem).start()
        return carry

    lax.fori_loop(0, rows, issue, 0, unroll=8)
    pltpu.make_async_copy(x_hbm.at[pl.ds(0, rows), :], buf_ref, sem).wait()
    o_ref[...] = buf_ref[...].astype(o_ref.dtype)


def _row_gather(x2, src, out_dtype):
    rg = src.shape[0]
    d = x2.shape[1]
    return pl.pallas_call(
        _row_gather_kernel,
        out_shape=jax.ShapeDtypeStruct((rg, d), out_dtype),
        grid=(rg // MOE_TG,),
        in_specs=[pl.BlockSpec((MOE_TG,), lambda i: (i,), memory_space=pltpu.SMEM),
                  pl.BlockSpec(memory_space=pl.ANY)],
        out_specs=pl.BlockSpec((MOE_TG, d), lambda i: (i, 0)),
        scratch_shapes=[pltpu.VMEM((MOE_TG, d), x2.dtype), pltpu.SemaphoreType.DMA(())],
        compiler_params=_params(("arbitrary",), 32),
        name="moe_gather",
    )(src, x2)


def _gmm_swiglu_kernel(te_ref, nu_ref, a_ref, w1_ref, w3_ref, o_ref):
    used = pl.program_id(1) < nu_ref[0]

    @pl.when(used)
    def _():
        a = a_ref[...]
        g = jnp.dot(a, w1_ref[0], preferred_element_type=f32)
        u = jnp.dot(a, w3_ref[0], preferred_element_type=f32)
        o_ref[...] = (_silu(g) * u).astype(o_ref.dtype)

    @pl.when(jnp.logical_not(used))
    def _():
        o_ref[...] = jnp.zeros_like(o_ref)


def _gmm_out_kernel(te_ref, nu_ref, a_ref, w_ref, g_ref, o_ref):
    used = pl.program_id(1) < nu_ref[0]

    @pl.when(used)
    def _():
        o_ref[...] = jnp.dot(a_ref[...], w_ref[0], preferred_element_type=f32) * g_ref[...]

    @pl.when(jnp.logical_not(used))
    def _():
        o_ref[...] = jnp.zeros_like(o_ref)


def _used(i, nu):
    return jnp.minimum(i, nu[0] - 1)


def _gmm_swiglu(xs, w1, w3, tile_expert, n_used, tn=1024):
    rg, k = xs.shape
    n = w1.shape[2]
    wspec = pl.BlockSpec((1, k, tn), lambda j, i, te, nu: (te[_used(i, nu)], 0, j))
    return pl.pallas_call(
        _gmm_swiglu_kernel,
        out_shape=jax.ShapeDtypeStruct((rg, n), bf16),
        grid_spec=pltpu.PrefetchScalarGridSpec(
            num_scalar_prefetch=2, grid=(n // tn, rg // MOE_TG),
            in_specs=[pl.BlockSpec((MOE_TG, k), lambda j, i, te, nu: (_used(i, nu), 0)), wspec, wspec],
            out_specs=pl.BlockSpec((MOE_TG, tn), lambda j, i, te, nu: (i, j))),
        compiler_params=_params(("parallel", "arbitrary"), 48),
        name="moe_up",
    )(tile_expert, n_used, xs, w1, w3)


def _gmm_out(hmid, w2, rowgate, tile_expert, n_used, tn=1024):
    rg, k = hmid.shape
    n = w2.shape[2]
    return pl.pallas_call(
        _gmm_out_kernel,
        out_shape=jax.ShapeDtypeStruct((rg, n), f32),
        grid_spec=pltpu.PrefetchScalarGridSpec(
            num_scalar_prefetch=2, grid=(n // tn, rg // MOE_TG),
            in_specs=[pl.BlockSpec((MOE_TG, k), lambda j, i, te, nu: (_used(i, nu), 0)),
                      pl.BlockSpec((1, k, tn), lambda j, i, te, nu: (te[_used(i, nu)], 0, j)),
                      pl.BlockSpec((MOE_TG, 1), lambda j, i, te, nu: (_used(i, nu), 0))],
            out_specs=pl.BlockSpec((MOE_TG, tn), lambda j, i, te, nu: (i, j))),
        compiler_params=_params(("parallel", "arbitrary"), 56),
        name="moe_down",
    )(tile_expert, n_used, hmid, w2, rowgate)


def _combine_kernel(p0_ref, p1_ref, ys_hbm, x_ref, m_ref, o_ref, b0_ref, b1_ref, sems, *, k_mod):
    rows = b0_ref.shape[0]

    def issue(r, carry):
        pltpu.make_async_copy(ys_hbm.at[pl.ds(p0_ref[r], 1), :], b0_ref.at[pl.ds(r, 1), :], sems.at[0]).start()
        pltpu.make_async_copy(ys_hbm.at[pl.ds(p1_ref[r], 1), :], b1_ref.at[pl.ds(r, 1), :], sems.at[1]).start()
        return carry

    lax.fori_loop(0, rows, issue, 0, unroll=8)
    pltpu.make_async_copy(ys_hbm.at[pl.ds(0, rows), :], b0_ref, sems.at[0]).wait()
    pltpu.make_async_copy(ys_hbm.at[pl.ds(0, rows), :], b1_ref, sems.at[1]).wait()
    o_ref[...] = x_ref[...] + m_ref[0, k_mod:k_mod + 1, :] * (b0_ref[...] + b1_ref[...])


def _moe_combine(ys, pos, x2, modtab, nb, t_all, l_lat, k_mod):
    r, d = x2.shape
    tm = _pick(math.gcd(l_lat, t_all - l_lat), (256, 128))
    tpb = t_all // tm
    n_lat_tiles = l_lat // tm
    sspec = pl.BlockSpec((tm,), lambda i: (i,), memory_space=pltpu.SMEM)
    return pl.pallas_call(
        functools.partial(_combine_kernel, k_mod=k_mod),
        out_shape=jax.ShapeDtypeStruct((r, d), f32),
        grid=(r // tm,),
        in_specs=[sspec, sspec, pl.BlockSpec(memory_space=pl.ANY),
                  pl.BlockSpec((tm, d), lambda i: (i, 0)),
                  pl.BlockSpec((1, 6, d), lambda i: (jnp.where(i % tpb >= n_lat_tiles, nb, i // tpb), 0, 0))],
        out_specs=pl.BlockSpec((tm, d), lambda i: (i, 0)),
        scratch_shapes=[pltpu.VMEM((tm, d), f32), pltpu.VMEM((tm, d), f32), pltpu.SemaphoreType.DMA((2,))],
        input_output_aliases={3: 0},
        compiler_params=_params(("arbitrary",), 32),
        name="moe_combine",
    )(pos[:, 0], pos[:, 1], ys, x2, modtab)


def _moe(fin32, xall, w_router, b_router, w1, w3, w2, modtab, nb, t_all, l_lat):
    rows = fin32.shape[0]
    rg = TOP_K * rows + N_EXPERTS * MOE_TG
    e_idx, wts = _router(fin32, w_router, b_router)
    pos, src, rowgate, tile_expert, n_used = _route_meta(e_idx, wts, rg)
    xs = _row_gather(fin32, src, bf16)
    hmid = _gmm_swiglu(xs, w1.astype(bf16), w3.astype(bf16), tile_expert, n_used)
    ys = _gmm_out(hmid, w2.astype(bf16), rowgate, tile_expert, n_used)
    return _moe_combine(ys, pos, xall, modtab, nb, t_all, l_lat, 5)


def _final_norm_kernel(x_ref, g_ref, o_ref):
    x = x_ref[0]
    o_ref[0] = x * lax.rsqrt(jnp.mean(x * x, axis=-1, keepdims=True) + NORM_EPS) * g_ref[...]


def _final_norm(x3, gain, l_lat):
    nb, t_all, d = x3.shape
    tr = _pick(math.gcd(l_lat, t_all), (512, 256, 128))
    return pl.pallas_call(
        _final_norm_kernel,
        out_shape=jax.ShapeDtypeStruct((nb, l_lat, d), f32),
        grid=(nb, l_lat // tr),
        in_specs=[pl.BlockSpec((1, tr, d), lambda b, j: (b, j, 0)),
                  pl.BlockSpec((1, d), lambda b, j: (0, 0))],
        out_specs=pl.BlockSpec((1, tr, d), lambda b, j: (b, j, 0)),
        compiler_params=_params(("parallel", "parallel"), 32),
        name="final_norm",
    )(x3, gain.reshape(1, d))


def _reorder_w_in(w):
    dt0 = C_SSD_XBC + SSD_XBC
    dt1 = dt0 + 2 * SSD_HEADS
    pad = jnp.zeros((w.shape[0], PROJ_W - w.shape[1]), w.dtype)
    return jnp.concatenate([w[:, :dt0], w[:, dt1:], w[:, dt0:dt1], pad], axis=1)


def _block_diag(w):
    h, bw, _ = w.shape
    eye = jnp.eye(h, dtype=w.dtype)
    return (eye[:, None, :, None] * w[:, :, None, :]).reshape(h * bw, h * bw)


def kernel(x, c, ctx, c_ctx, norm_mix, norm_ffn, ada_w, ada_b, w_in, hy_conv_w, hy_conv_b, hy_w1, hy_b1, hy_w2, hy_b2, hy_freq, hy_w3, hy_b3, hy_decay, hy_bias, lru_conv_w, lru_conv_b, lru_wa, lru_ba, lru_wx, lru_bx, lru_lambda, ssd_conv_w, ssd_conv_b, ssd_a_log, ssd_dt_bias, ssd_d, ssd_norm, att_q_norm, att_k_norm, w_branch, w_gate, b_gate, w_out, ffn_w1, ffn_w3, ffn_w2, moe_router, moe_router_b, moe_w1, moe_w3, moe_w2, norm_final):
    nb, l_lat, d = x.shape
    lc = ctx.shape[1]
    t_all = l_lat + lc
    rows = nb * t_all
    depth = norm_mix.shape[0]
    tm = _pick(t_all, (768, 384, 256, 128))

    mod_rows = -(-(nb + 1) // SUBLANES) * SUBLANES
    cvec = jnp.zeros((mod_rows, d), f32).at[:nb].set(c).at[nb].set(c_ctx)
    mods = _mods(cvec, ada_w, ada_b).reshape(depth, mod_rows, 6, d)

    cos2, sin2 = _rope_tables(l_lat, lc)
    dft_l, cs_l, nf_l = _dft_mats(l_lat)
    dft_c, cs_c, nf_c = _dft_mats(lc)

    xall = jnp.concatenate([x, ctx], axis=1).reshape(rows, d)

    for l in range(depth):
        with_ctx = l < depth - 1
        modtab = mods[l]
        h = _normmod(xall, norm_mix[l], modtab, nb, t_all, l_lat, 0, 1)
        proj2 = _mm(h, _reorder_w_in(w_in[l]).astype(bf16), f32, tm, 640)
        proj = proj2.reshape(nb, t_all, PROJ_W)

        hy_args = (hy_w1[l], hy_b1[l], hy_w2[l], hy_b2[l], hy_freq[l], hy_w3[l], hy_b3[l], hy_decay[l], hy_bias[l])
        hu = _dwconv(proj, C_HY, HY_IN, hy_conv_w[l], hy_conv_b[l], (HY_SHORT - 1) // 2, l_lat)
        k_l = _hyena_kspec(l_lat, cs_l, nf_l, *hy_args)
        z_l = _hyconv(hu, (0, 0), hu, (0, 1), l_lat, dft_l, k_l[0])
        hy_l = _hyconv(z_l, (0, 0), hu, (0, 2), l_lat, dft_l, k_l[1])
        if with_ctx:
            k_c = _hyena_kspec(lc, cs_c, nf_c, *hy_args)
            z_c = _hyconv(hu, (l_lat // lc, 0), hu, (l_lat // lc, 1), lc, dft_c, k_c[0])
            hy_c = _hyconv(z_c, (0, 0), hu, (l_lat // lc, 2), lc, dft_c, k_c[1])
        else:
            hy_c = jnp.zeros((nb, lc, HY_W), bf16)
        y_hy = jnp.concatenate([hy_l, hy_c], axis=1).reshape(rows, HY_W)

        xc = _dwconv(proj, C_LRU_X, LRU_W, lru_conv_w[l], lru_conv_b[l], LRU_CONV // 2, l_lat)
        w_all = jnp.concatenate([_block_diag(lru_wa[l, 0]), _block_diag(lru_wx[l, 0]),
                                 _block_diag(lru_wa[l, 1]), _block_diag(lru_wx[l, 1])], axis=1).astype(bf16)
        bias_all = jnp.concatenate([lru_ba[l, 0].reshape(-1), lru_bx[l, 0].reshape(-1),
                                    lru_ba[l, 1].reshape(-1), lru_bx[l, 1].reshape(-1)]).reshape(1, 4 * LRU_W)
        a2, b2 = _lru_coefs(xc.reshape(rows, LRU_W), w_all, bias_all, lru_lambda[l])
        hf, hb = _lru_scan(a2.reshape(2, nb, t_all, LRU_W), b2.reshape(2, nb, t_all, LRU_W), l_lat)
        y_lru = _lru_finish(proj2, hf.reshape(rows, LRU_W), hb.reshape(rows, LRU_W))

        xbc = _dwconv(proj, C_SSD_XBC, SSD_XBC, ssd_conv_w[l], ssd_conv_b[l], SSD_CONV // 2, l_lat, act=True)
        dt_t = proj[:, :, C_DT:C_DT + 2 * SSD_HEADS].transpose(0, 2, 1)
        yf, yb = _ssd_scan(xbc, proj, dt_t, ssd_a_log[l], ssd_dt_bias[l], l_lat)
        y_ssd = _ssd_finish(yf.reshape(rows, SSD_W), yb.reshape(rows, SSD_W), xbc.reshape(rows, SSD_XBC),
                            proj2, ssd_d[l], ssd_norm[l])

        y_att = _attention(proj, cos2, sin2, att_q_norm[l], att_k_norm[l], l_lat).reshape(rows, ATT_Q)

        wb = w_branch[l].astype(bf16)
        offs = (0, HY_W, HY_W + LRU_W, HY_W + LRU_W + SSD_W, HY_W + LRU_W + SSD_W + ATT_Q)
        wbs = [wb[offs[i]:offs[i + 1]] for i in range(4)]
        merged = _merge(h, [y_hy, y_lru, y_ssd, y_att], w_gate[l].astype(bf16), b_gate[l], wbs, tm)
        xall = _mm_resid(merged, w_out[l].astype(bf16), xall, modtab, nb, t_all, l_lat, 2, 1024)

        j = l // 2
        if l % 2 == 0:
            fin = _normmod(xall, norm_ffn[l], modtab, nb, t_all, l_lat, 3, 4)
            padc = FF_DENSE_PAD - FF_DENSE
            w1 = jnp.pad(ffn_w1[j], ((0, 0), (0, padc))).astype(bf16)
            w3 = jnp.pad(ffn_w3[j], ((0, 0), (0, padc))).astype(bf16)
            w2 = jnp.pad(ffn_w2[j], ((0, padc), (0, 0))).astype(bf16)
            hmid = _mm_swiglu(fin, w1, w3, tm, 512)
            xall = _mm_resid(hmid, w2, xall, modtab, nb, t_all, l_lat, 5, 512)
        else:
            fin32 = _normmod(xall, norm_ffn[l], modtab, nb, t_all, l_lat, 3, 4, out_dtype=f32)
            xall = _moe(fin32, xall, moe_router[j], moe_router_b[j], moe_w1[j], moe_w3[j], moe_w2[j],
                        modtab, nb, t_all, l_lat)

    return _final_norm(xall.reshape(nb, t_all, d), norm_final, l_lat)
```

```python
import functools
import math

import jax
import jax.numpy as jnp
from jax import lax
from jax.experimental import pallas as pl
from jax.experimental.pallas import tpu as pltpu

f32 = jnp.float32
bf16 = jnp.bfloat16

D_MODEL = 2048
DEPTH = 4
GRID_W = 64
NORM_EPS = 1e-6
HY_W = 512
HY_ORDER = 2
HY_SHORT = 3
HY_BANDS = 16
HY_IN = (HY_ORDER + 1) * HY_W
LRU_W = 512
LRU_HEADS = 8
LRU_BW = LRU_W // LRU_HEADS
LRU_CONV = 4
LRU_C = 8.0
SSD_HEADS = 8
SSD_HEAD_DIM = 64
SSD_W = SSD_HEADS * SSD_HEAD_DIM
SSD_GROUPS = 2
SSD_STATE = 128
SSD_CONV = 4
SSD_CHUNK = 128
SSD_XBC = SSD_W + 2 * SSD_GROUPS * SSD_STATE
ATT_HEADS = 8
ATT_KV_HEADS = 2
ATT_HEAD_DIM = 128
ROPE_THETA = 10000.0
ATT_Q = ATT_HEADS * ATT_HEAD_DIM
ATT_KV = ATT_KV_HEADS * ATT_HEAD_DIM
FF_DENSE = 5504
N_EXPERTS = 8
TOP_K = 2

LANES = 128
SUBLANES = 8
MIB = 2 ** 20

C_HY = 0
C_LRU_G = C_HY + HY_IN
C_LRU_X = C_LRU_G + LRU_W
C_SSD_Z = C_LRU_X + LRU_W
C_SSD_XBC = C_SSD_Z + SSD_W
C_Q = C_SSD_XBC + SSD_XBC
C_K = C_Q + ATT_Q
C_V = C_K + ATT_KV
PROJ_W = C_V + ATT_KV
FF_DENSE_PAD = 5632
GATE_LANES = LANES


def _params(sem, vmem_mib):
    return pltpu.CompilerParams(dimension_semantics=sem, vmem_limit_bytes=vmem_mib * MIB)


def _pick(n, cands):
    for c in cands:
        if n % c == 0:
            return c
    raise ValueError(f"no tile for {n} in {cands}")


def _silu(v):
    return v * jax.nn.sigmoid(v)


def _softplus(v):
    return jnp.maximum(v, 0.0) + jnp.log1p(jnp.exp(-jnp.abs(v)))


def _mods_kernel(c_ref, w_ref, b_ref, o_ref):
    s = _silu(c_ref[...]).astype(bf16)
    o_ref[0] = jnp.dot(s, w_ref[0].astype(bf16), preferred_element_type=f32) + b_ref[0]


def _mods(cvec, ada_w, ada_b):
    depth, d, n = ada_w.shape
    rows = cvec.shape[0]
    tn = 1024
    return pl.pallas_call(
        _mods_kernel,
        out_shape=jax.ShapeDtypeStruct((depth, rows, n), f32),
        grid=(depth, n // tn),
        in_specs=[pl.BlockSpec((rows, d), lambda l, j: (0, 0)),
                  pl.BlockSpec((1, d, tn), lambda l, j: (l, 0, j)),
                  pl.BlockSpec((1, 1, tn), lambda l, j: (l, 0, j))],
        out_specs=pl.BlockSpec((1, rows, tn), lambda l, j: (l, 0, j)),
        compiler_params=_params(("parallel", "parallel"), 40),
        name="mods",
    )(cvec, ada_w, ada_b.reshape(depth, 1, n))


def _row_mod(ml_ref, mc_ref, k, is_ctx):
    return jnp.where(is_ctx, mc_ref[0, k:k + 1, :], ml_ref[0, k:k + 1, :])


def _is_ctx_rows(tm, tpb, l_lat, axis):
    t = (pl.program_id(axis) % tpb) * tm + lax.broadcasted_iota(jnp.int32, (tm, 1), 0)
    return t >= l_lat


def _normmod_kernel(x_ref, g_ref, ml_ref, mc_ref, o_ref, *, tm, tpb, l_lat, k_shift, k_scale):
    x = x_ref[...]
    y = x * lax.rsqrt(jnp.mean(x * x, axis=-1, keepdims=True) + NORM_EPS) * g_ref[...]
    is_ctx = _is_ctx_rows(tm, tpb, l_lat, 0)
    shift = _row_mod(ml_ref, mc_ref, k_shift, is_ctx)
    scale = _row_mod(ml_ref, mc_ref, k_scale, is_ctx)
    o_ref[...] = (y * (1.0 + scale) + shift).astype(o_ref.dtype)


def _normmod(x2, gain, modtab, nb, t_all, l_lat, k_shift, k_scale, out_dtype=bf16):
    r, d = x2.shape
    tm = _pick(t_all, (768, 384, 256, 128))
    tpb = t_all // tm
    return pl.pallas_call(
        functools.partial(_normmod_kernel, tm=tm, tpb=tpb, l_lat=l_lat, k_shift=k_shift, k_scale=k_scale),
        out_shape=jax.ShapeDtypeStruct((r, d), out_dtype),
        grid=(r // tm,),
        in_specs=[pl.BlockSpec((tm, d), lambda i: (i, 0)),
                  pl.BlockSpec((1, d), lambda i: (0, 0)),
                  pl.BlockSpec((1, 6, d), lambda i: (i // tpb, 0, 0)),
                  pl.BlockSpec((1, 6, d), lambda i: (nb, 0, 0))],
        out_specs=pl.BlockSpec((tm, d), lambda i: (i, 0)),
        compiler_params=_params(("parallel",), 56),
        name="normmod",
    )(x2, gain.reshape(1, d), modtab, modtab)


def _cast_on_first_row_tile(pairs):
    @pl.when(pl.program_id(1) == 0)
    def _():
        for src, dst in pairs:
            dst[...] = src[...].astype(bf16)


def _mm_kernel(a_ref, w_ref, o_ref, wb_ref):
    _cast_on_first_row_tile(((w_ref, wb_ref),))
    o_ref[...] = jnp.dot(a_ref[...], wb_ref[...], preferred_element_type=f32).astype(o_ref.dtype)


def _mm(a, w, out_dtype, tm, tn, vmem=48):
    m, k = a.shape
    n = w.shape[1]
    return pl.pallas_call(
        _mm_kernel,
        out_shape=jax.ShapeDtypeStruct((m, n), out_dtype),
        grid=(n // tn, m // tm),
        in_specs=[pl.BlockSpec((tm, k), lambda j, i: (i, 0)),
                  pl.BlockSpec((k, tn), lambda j, i: (0, j))],
        out_specs=pl.BlockSpec((tm, tn), lambda j, i: (i, j)),
        scratch_shapes=[pltpu.VMEM((k, tn), bf16)],
        compiler_params=_params(("parallel", "arbitrary"), vmem),
        name="mm",
    )(a, w)


def _mm_swiglu_kernel(a_ref, w1_ref, w3_ref, o_ref, w1b_ref, w3b_ref):
    _cast_on_first_row_tile(((w1_ref, w1b_ref), (w3_ref, w3b_ref)))
    a = a_ref[...]
    g = jnp.dot(a, w1b_ref[...], preferred_element_type=f32)
    u = jnp.dot(a, w3b_ref[...], preferred_element_type=f32)
    o_ref[...] = (_silu(g) * u).astype(o_ref.dtype)


def _mm_swiglu(a, w1, w3, tm, tn, vmem=48):
    m, k = a.shape
    n = w1.shape[1]
    wspec = pl.BlockSpec((k, tn), lambda j, i: (0, j))
    return pl.pallas_call(
        _mm_swiglu_kernel,
        out_shape=jax.ShapeDtypeStruct((m, n), bf16),
        grid=(n // tn, m // tm),
        in_specs=[pl.BlockSpec((tm, k), lambda j, i: (i, 0)), wspec, wspec],
        out_specs=pl.BlockSpec((tm, tn), lambda j, i: (i, j)),
        scratch_shapes=[pltpu.VMEM((k, tn), bf16)] * 2,
        compiler_params=_params(("parallel", "arbitrary"), vmem),
        name="mm_swiglu",
    )(a, w1, w3)


def _mm_resid_kernel(a_ref, w_ref, x_ref, ml_ref, mc_ref, o_ref, wb_ref, *, tm, tpb, l_lat, k_mod):
    _cast_on_first_row_tile(((w_ref, wb_ref),))
    acc = jnp.dot(a_ref[...], wb_ref[...], preferred_element_type=f32)
    mod = _row_mod(ml_ref, mc_ref, k_mod, _is_ctx_rows(tm, tpb, l_lat, 1))
    o_ref[...] = x_ref[...] + mod * acc


def _mm_resid(a, w, x2, modtab, nb, t_all, l_lat, k_mod, tm, tn, vmem=48):
    m, k = a.shape
    n = w.shape[1]
    tpb = t_all // tm
    return pl.pallas_call(
        functools.partial(_mm_resid_kernel, tm=tm, tpb=tpb, l_lat=l_lat, k_mod=k_mod),
        out_shape=jax.ShapeDtypeStruct((m, n), f32),
        grid=(n // tn, m // tm),
        in_specs=[pl.BlockSpec((tm, k), lambda j, i: (i, 0)),
                  pl.BlockSpec((k, tn), lambda j, i: (0, j)),
                  pl.BlockSpec((tm, tn), lambda j, i: (i, j)),
                  pl.BlockSpec((1, 6, tn), lambda j, i: (i // tpb, 0, j)),
                  pl.BlockSpec((1, 6, tn), lambda j, i: (nb, 0, j))],
        out_specs=pl.BlockSpec((tm, tn), lambda j, i: (i, j)),
        scratch_shapes=[pltpu.VMEM((k, tn), bf16)],
        input_output_aliases={2: 0},
        compiler_params=_params(("parallel", "arbitrary"), vmem),
        name="mm_resid",
    )(a, w, x2, modtab, modtab)


def _merge_kernel(h_ref, y0_ref, y1_ref, y2_ref, y3_ref, wg_ref, bg_ref, wb_ref, o_ref, wgb_ref, wbb_ref,
                  *, offs):
    _cast_on_first_row_tile(((wg_ref, wgb_ref), (wb_ref, wbb_ref)))
    h = h_ref[...]
    acc = None
    for i, y_ref in enumerate((y0_ref, y1_ref, y2_ref, y3_ref)):
        g = jnp.dot(h, wgb_ref[i], preferred_element_type=f32) + bg_ref[i]
        p = jnp.dot(y_ref[...], wbb_ref[offs[i]:offs[i + 1], :], preferred_element_type=f32)
        term = jax.nn.sigmoid(g) * p
        acc = term if acc is None else acc + term
    o_ref[...] = acc.astype(o_ref.dtype)


def _merge(h, ys, wg, bg, wb, tm, tn=256):
    m, d = h.shape
    n = wg.shape[2]
    offs = [0]
    for y in ys:
        offs.append(offs[-1] + y.shape[1])
    in_specs = [pl.BlockSpec((tm, d), lambda j, i: (i, 0))]
    in_specs += [pl.BlockSpec((tm, y.shape[1]), lambda j, i: (i, 0)) for y in ys]
    in_specs += [pl.BlockSpec((4, d, tn), lambda j, i: (0, 0, j)),
                 pl.BlockSpec((4, 1, tn), lambda j, i: (0, 0, j)),
                 pl.BlockSpec((wb.shape[0], tn), lambda j, i: (0, j))]
    return pl.pallas_call(
        functools.partial(_merge_kernel, offs=tuple(offs)),
        out_shape=jax.ShapeDtypeStruct((m, n), bf16),
        grid=(n // tn, m // tm),
        in_specs=in_specs,
        out_specs=pl.BlockSpec((tm, tn), lambda j, i: (i, j)),
        scratch_shapes=[pltpu.VMEM((4, d, tn), bf16), pltpu.VMEM((wb.shape[0], tn), bf16)],
        compiler_params=_params(("parallel", "arbitrary"), 56),
        name="merge",
    )(h, *ys, wg, bg.reshape(4, 1, n), wb)


def _norm_rope(x, gain, cos, sin):
    y = x * lax.rsqrt(jnp.mean(x * x, axis=-1, keepdims=True) + NORM_EPS) * gain
    lane = lax.broadcasted_iota(jnp.int32, y.shape, 1)
    partner = jnp.where(lane % 2 == 0, pltpu.roll(y, ATT_HEAD_DIM - 1, 1), pltpu.roll(y, 1, 1))
    return y * cos + partner * sin


def _attn_kernel(q_ref, k_ref, v_ref, cos_ref, sin_ref, qg_ref, kg_ref, o_ref, ks_ref, vs_ref,
                 *, tq, l_lat, n_lat_blocks):
    qi = pl.program_id(2)
    grp = ATT_HEADS // ATT_KV_HEADS
    hd = ATT_HEAD_DIM

    @pl.when(qi == 0)
    def _():
        ks_ref[...] = _norm_rope(k_ref[0].astype(f32), kg_ref[...], cos_ref[...], sin_ref[...]).astype(bf16)
        vs_ref[:, :hd] = v_ref[0].astype(bf16)
        vs_ref[:, hd:] = jnp.ones((vs_ref.shape[0], hd), bf16)

    r0 = pl.multiple_of(qi * tq, tq)
    cos = cos_ref[pl.ds(r0, tq), :]
    sin = sin_ref[pl.ds(r0, tq), :]
    q = q_ref[0].astype(f32)
    sc = (hd ** -0.5) * math.log2(math.e)
    qhs = [(_norm_rope(q[:, h * hd:(h + 1) * hd], qg_ref[...], cos, sin) * sc).astype(bf16) for h in range(grp)]

    def attend(ks, vs):
        for h in range(grp):
            s = lax.dot_general(qhs[h], ks, (((1,), (1,)), ((), ())), preferred_element_type=f32)
            p = jnp.exp2(s - jnp.max(s, axis=-1, keepdims=True)).astype(bf16)
            oe = jnp.dot(p, vs, preferred_element_type=f32)
            o_ref[0, :, h * hd:(h + 1) * hd] = (oe[:, :hd] / oe[:, hd:hd + 1]).astype(o_ref.dtype)

    @pl.when(qi < n_lat_blocks)
    def _():
        attend(ks_ref[...], vs_ref[...])

    @pl.when(qi >= n_lat_blocks)
    def _():
        attend(ks_ref[l_lat:, :], vs_ref[l_lat:, :])


def _attention(proj, cos2, sin2, q_gain, k_gain, l_lat):
    nb, t_all, _ = proj.shape
    lc = t_all - l_lat
    tq = lc if (l_lat % lc == 0 and lc <= 256) else _pick(math.gcd(l_lat, lc), (256, 128))
    grp = ATT_HEADS // ATT_KV_HEADS
    qw = grp * ATT_HEAD_DIM
    hd = ATT_HEAD_DIM
    return pl.pallas_call(
        functools.partial(_attn_kernel, tq=tq, l_lat=l_lat, n_lat_blocks=l_lat // tq),
        out_shape=jax.ShapeDtypeStruct((nb, t_all, ATT_Q), bf16),
        grid=(nb, ATT_KV_HEADS, t_all // tq),
        in_specs=[pl.BlockSpec((1, tq, qw), lambda b, g, i: (b, i, C_Q // qw + g)),
                  pl.BlockSpec((1, t_all, hd), lambda b, g, i: (b, 0, C_K // hd + g)),
                  pl.BlockSpec((1, t_all, hd), lambda b, g, i: (b, 0, C_V // hd + g)),
                  pl.BlockSpec((t_all, hd), lambda b, g, i: (0, 0)),
                  pl.BlockSpec((t_all, hd), lambda b, g, i: (0, 0)),
                  pl.BlockSpec((1, hd), lambda b, g, i: (0, 0)),
                  pl.BlockSpec((1, hd), lambda b, g, i: (0, 0))],
        out_specs=pl.BlockSpec((1, tq, qw), lambda b, g, i: (b, i, g)),
        scratch_shapes=[pltpu.VMEM((t_all, hd), bf16), pltpu.VMEM((t_all, 2 * hd), bf16)],
        compiler_params=_params(("parallel", "parallel", "arbitrary"), 56),
        name="attention",
    )(proj, proj, proj, cos2, sin2, q_gain.reshape(1, hd), k_gain.reshape(1, hd))


def _rope_tables(l_lat, lc):
    axis_dim = ATT_HEAD_DIM // 2
    rows = l_lat // GRID_W
    row_id = jnp.repeat(jnp.arange(rows, dtype=f32), GRID_W)
    col_id = jnp.tile(jnp.arange(GRID_W, dtype=f32), rows)
    inv = ROPE_THETA ** (-jnp.arange(0, axis_dim, 2, dtype=f32) / axis_dim)
    ang = jnp.concatenate([row_id[:, None] * inv, col_id[:, None] * inv], axis=-1)
    cos = jnp.repeat(jnp.cos(ang), 2, axis=-1)
    sin = jnp.repeat(jnp.sin(ang), 2, axis=-1)
    sign = jnp.tile(jnp.array([-1.0, 1.0], f32), ATT_HEAD_DIM // 2)
    cos2 = jnp.concatenate([cos, jnp.ones((lc, ATT_HEAD_DIM), f32)], axis=0)
    sin2 = jnp.concatenate([sin * sign, jnp.zeros((lc, ATT_HEAD_DIM), f32)], axis=0)
    return cos2, sin2


def _dwconv_kernel(u_ref, w_ref, b_ref, o_ref, *, ksz, pad_left, l_lat, t_all, act):
    u = u_ref[0].astype(f32)
    t = lax.broadcasted_iota(jnp.int32, (t_all, 1), 0)
    is_ctx = t >= l_lat
    t_loc = jnp.where(is_ctx, t - l_lat, t)
    seg_len = jnp.where(is_ctx, t_all - l_lat, l_lat)
    acc = None
    for k in range(ksz):
        s = k - pad_left
        if s == 0:
            term = u
        else:
            shifted = pltpu.roll(u, (t_all - s) % t_all, 0)
            term = jnp.where((t_loc + s >= 0) & (t_loc + s < seg_len), shifted, 0.0)
        term = term * w_ref[k:k + 1, :]
        acc = term if acc is None else acc + term
    acc = acc + b_ref[...]
    if act:
        acc = _silu(acc)
    o_ref[0] = acc.astype(o_ref.dtype)


def _dwconv(proj, col0, width, w, b, pad_left, l_lat, act=False):
    nb, t_all, _ = proj.shape
    ksz = w.shape[0]
    cw = 256
    return pl.pallas_call(
        functools.partial(_dwconv_kernel, ksz=ksz, pad_left=pad_left, l_lat=l_lat, t_all=t_all, act=act),
        out_shape=jax.ShapeDtypeStruct((nb, t_all, width), f32),
        grid=(nb, width // cw),
        in_specs=[pl.BlockSpec((1, t_all, cw), lambda bi, c: (bi, 0, col0 // cw + c)),
                  pl.BlockSpec((ksz, cw), lambda bi, c: (0, c)),
                  pl.BlockSpec((1, cw), lambda bi, c: (0, c))],
        out_specs=pl.BlockSpec((1, t_all, cw), lambda bi, c: (bi, 0, c)),
        compiler_params=_params(("parallel", "parallel"), 48),
        name="dwconv",
    )(proj, w, b.reshape(1, width))


def _hyconv_kernel(u_ref, g_ref, m_ref, mi_ref, k_ref, o_ref, ub_ref, acc_ref, *, fc, nf):
    f = pl.program_id(2)

    @pl.when(f == 0)
    def _():
        ub_ref[...] = u_ref[0].astype(bf16)
        acc_ref[...] = jnp.zeros_like(acc_ref)

    spec = jnp.dot(m_ref[0], ub_ref[...], preferred_element_type=f32)
    a, bv = spec[:fc], spec[fc:]
    kk = k_ref[0]
    kr, ki = kk[:fc], kk[fc:]
    y = jnp.concatenate([a * kr + bv * ki, a * ki - bv * kr], axis=0).astype(bf16)
    acc_ref[...] += jnp.dot(mi_ref[0], y, preferred_element_type=f32)

    @pl.when(f == nf - 1)
    def _():
        o_ref[0] = (acc_ref[...] * g_ref[0]).astype(o_ref.dtype)


def _hyconv(u, u_blk, gate, gate_blk, lseg, dft, kspec):
    m_mat, mi_mat = dft
    nf, fc2, _ = m_mat.shape
    fc = fc2 // 2
    nb = u.shape[0]
    cw = HY_W
    return pl.pallas_call(
        functools.partial(_hyconv_kernel, fc=fc, nf=nf),
        out_shape=jax.ShapeDtypeStruct((nb, lseg, cw), bf16),
        grid=(nb, 1, nf),
        in_specs=[pl.BlockSpec((1, lseg, cw), lambda b, c, f: (b,) + u_blk),
                  pl.BlockSpec((1, lseg, cw), lambda b, c, f: (b,) + gate_blk),
                  pl.BlockSpec((1, fc2, lseg), lambda b, c, f: (f, 0, 0)),
                  pl.BlockSpec((1, lseg, fc2), lambda b, c, f: (f, 0, 0)),
                  pl.BlockSpec((1, fc2, cw), lambda b, c, f: (f, 0, 0))],
        out_specs=pl.BlockSpec((1, lseg, cw), lambda b, c, f: (b, 0, 0)),
        scratch_shapes=[pltpu.VMEM((lseg, cw), bf16), pltpu.VMEM((lseg, cw), f32)],
        compiler_params=_params(("parallel", "arbitrary", "arbitrary"), 52),
        name="hyconv",
    )(u, gate, m_mat, mi_mat, kspec)


def _dft_cs(lseg):
    n = 2 * lseg
    f = jnp.arange(lseg, dtype=jnp.int32)[:, None]
    s = jnp.arange(lseg, dtype=jnp.int32)[None, :]
    ph = ((2 * f + 1) * s) % (2 * n)
    ang = ph.astype(f32) * (math.pi / n)
    return jnp.cos(ang), jnp.sin(ang)


def _dft_mats(lseg):
    fc = min(256, lseg // 2)
    nf = lseg // fc
    c, s = _dft_cs(lseg)
    c3 = c.reshape(nf, fc, lseg)
    s3 = s.reshape(nf, fc, lseg)
    m_mat = jnp.concatenate([c3, s3], axis=1).astype(bf16)
    scale = 1.0 / lseg
    mi_mat = (jnp.concatenate([c3, -s3], axis=1) * scale).transpose(0, 2, 1).astype(bf16)
    return (m_mat, mi_mat), (c, s), (nf, fc)


def _hyena_filters(lseg, w1, b1, w2, b2, freq, w3, b3, decay):
    hi = lax.Precision.HIGHEST
    t = jnp.arange(lseg, dtype=f32)
    tn = t / lseg
    bands = jnp.linspace(1e-4, HY_BANDS - 1, HY_BANDS, dtype=f32)
    ang = (2.0 * math.pi / lseg) * t[:, None] * bands[None, :]
    feat = jnp.concatenate([tn[:, None], jnp.cos(ang), -jnp.sin(ang)], axis=-1)
    hdn = jnp.sin(freq[0] * (jnp.dot(feat, w1, precision=hi) + b1))
    hdn = jnp.sin(freq[1] * (jnp.dot(hdn, w2, precision=hi) + b2))
    filt = (jnp.dot(hdn, w3, precision=hi) + b3).reshape(lseg, HY_ORDER, 2, HY_W)
    window = jnp.exp(-tn[:, None, None, None] * jnp.abs(decay)[None])
    return filt * window


def _hyena_kspec(lseg, cs, nf_fc, w1, b1, w2, b2, freq, w3, b3, decay, bias):
    hi = lax.Precision.HIGHEST
    c, s = cs
    nf, fc = nf_fc
    filt = _hyena_filters(lseg, w1, b1, w2, b2, freq, w3, b3, decay)
    out = []
    for o in range(HY_ORDER):
        hf = filt[:, o, 0].at[0].add(bias[o])
        hb = filt[:, o, 1].at[0].set(0.0)
        kr = jnp.dot(c, hf + hb, precision=hi).reshape(nf, fc, HY_W)
        ki = jnp.dot(s, hb - hf, precision=hi).reshape(nf, fc, HY_W)
        out.append(jnp.concatenate([kr, ki], axis=1))
    return out


def _lru_coef_kernel(xc_ref, w_ref, bias_ref, lam_ref, a_ref, b_ref):
    xc = xc_ref[...]
    z = jnp.dot(xc.astype(bf16), w_ref[...], preferred_element_type=f32) + bias_ref[...]
    g = jax.nn.sigmoid(z)
    for d in range(2):
        r = g[:, d * 2 * LRU_W:d * 2 * LRU_W + LRU_W]
        i = g[:, d * 2 * LRU_W + LRU_W:(d + 1) * 2 * LRU_W]
        log_a = -LRU_C * r * _softplus(-lam_ref[d:d + 1, :])
        th = jnp.tanh(log_a)
        one_minus_a2 = -2.0 * th / (1.0 - th)
        a_ref[d] = jnp.exp(log_a)
        b_ref[d] = jnp.sqrt(one_minus_a2) * i * xc


def _lru_coefs(xc2, w_all, bias_all, lam):
    r, w = xc2.shape
    tm = _pick(r, (512, 384, 256, 128))
    return pl.pallas_call(
        _lru_coef_kernel,
        out_shape=[jax.ShapeDtypeStruct((2, r, w), f32)] * 2,
        grid=(r // tm,),
        in_specs=[pl.BlockSpec((tm, w), lambda i: (i, 0)),
                  pl.BlockSpec((w, 4 * w), lambda i: (0, 0)),
                  pl.BlockSpec((1, 4 * w), lambda i: (0, 0)),
                  pl.BlockSpec((2, w), lambda i: (0, 0))],
        out_specs=[pl.BlockSpec((2, tm, w), lambda i: (0, i, 0))] * 2,
        compiler_params=_params(("parallel",), 40),
        name="lru_coefs",
    )(xc2, w_all, bias_all, lam)


def _lru_scan_kernel(af_ref, bf_ref, ab_ref, bb_ref, hf_ref, hb_ref, st_ref, *, blk):
    @pl.when(pl.program_id(1) == 0)
    def _():
        st_ref[...] = jnp.zeros_like(st_ref)

    def body(g, carry):
        hf, hb = carry
        base = pl.multiple_of(g * SUBLANES, SUBLANES)
        rbase = pl.multiple_of(blk - SUBLANES - g * SUBLANES, SUBLANES)
        for r in range(SUBLANES):
            hf = af_ref[0, 0, pl.ds(base + r, 1), :] * hf + bf_ref[0, 0, pl.ds(base + r, 1), :]
            hf_ref[0, pl.ds(base + r, 1), :] = hf
            rr = rbase + (SUBLANES - 1 - r)
            hb = ab_ref[0, 0, pl.ds(rr, 1), :] * hb + bb_ref[0, 0, pl.ds(rr, 1), :]
            hb_ref[0, pl.ds(rr, 1), :] = hb
        return hf, hb

    hf, hb = lax.fori_loop(0, blk // SUBLANES, body, (st_ref[0:1, :], st_ref[1:2, :]))
    st_ref[0:1, :] = hf
    st_ref[1:2, :] = hb


def _seg_orders(n_lat, n_ctx):
    def fwd(i):
        return jnp.where(i < n_ctx, n_lat + i, i - n_ctx)

    def bwd(i):
        return jnp.where(i < n_ctx, n_lat + n_ctx - 1 - i, n_lat - 1 - (i - n_ctx))

    return fwd, bwd


def _lru_scan(a4, b4, l_lat):
    _, nb, t_all, w = a4.shape
    lc = t_all - l_lat
    blk = _pick(math.gcd(l_lat, lc), (256, 128))
    fwd, bwd = _seg_orders(l_lat // blk, lc // blk)
    spec_f = pl.BlockSpec((1, 1, blk, w), lambda b, i: (0, b, fwd(i), 0))
    spec_b = pl.BlockSpec((1, 1, blk, w), lambda b, i: (1, b, bwd(i), 0))
    return pl.pallas_call(
        functools.partial(_lru_scan_kernel, blk=blk),
        out_shape=[jax.ShapeDtypeStruct((nb, t_all, w), f32)] * 2,
        grid=(nb, t_all // blk),
        in_specs=[spec_f, spec_f, spec_b, spec_b],
        out_specs=[pl.BlockSpec((1, blk, w), lambda b, i: (b, fwd(i), 0)),
                   pl.BlockSpec((1, blk, w), lambda b, i: (b, bwd(i), 0))],
        scratch_shapes=[pltpu.VMEM((SUBLANES, w), f32)],
        compiler_params=_params(("parallel", "arbitrary"), 32),
        name="lru_scan",
    )(a4, b4, a4, b4)


def _lru_finish_kernel(g_ref, hf_ref, hb_ref, o_ref):
    g = g_ref[...].astype(f32)
    gelu = 0.5 * g * (1.0 + jnp.tanh(math.sqrt(2.0 / math.pi) * (g + 0.044715 * (g * g * g))))
    o_ref[...] = (gelu * (hf_ref[...] + hb_ref[...])).astype(o_ref.dtype)


def _lru_finish(proj2, hf2, hb2):
    r, w = hf2.shape
    tm = _pick(r, (1024, 768, 512, 384, 256, 128))
    return pl.pallas_call(
        _lru_finish_kernel,
        out_shape=jax.ShapeDtypeStruct((r, w), bf16),
        grid=(r // tm,),
        in_specs=[pl.BlockSpec((tm, w), lambda i: (i, C_LRU_G // LRU_W)),
                  pl.BlockSpec((tm, w), lambda i: (i, 0)),
                  pl.BlockSpec((tm, w), lambda i: (i, 0))],
        out_specs=pl.BlockSpec((tm, w), lambda i: (i, 0)),
        compiler_params=_params(("parallel",), 32),
        name="lru_finish",
    )(proj2, hf2, hb2)


def _ssd_dir(d, x_ref, dc_ref, dr_ref, tri_ref, trit_ref, alog_l_ref, dtb_l_ref, alog_c_ref, dtb_c_ref,
             y_ref, st_ref):
    hi = lax.Precision.HIGHEST
    q = SSD_CHUNK
    p = SSD_HEAD_DIM
    ns = SSD_STATE
    per_g = SSD_HEADS // SSD_GROUPS
    xbc = x_ref[0]
    tri = tri_ref[d]
    mask = tri > 0.5
    dt_c = _softplus(dc_ref[0] + dtb_l_ref[...])
    dta_c = dt_c * (-jnp.exp(alog_l_ref[...]))
    cs_c = jnp.dot(tri, dta_c, precision=hi, preferred_element_type=f32)
    dt_r = _softplus(dr_ref[0] + dtb_c_ref[...])
    dta_r = dt_r * (-jnp.exp(alog_c_ref[...]))
    cs_r = jnp.dot(dta_r, trit_ref[d], precision=hi, preferred_element_type=f32)
    tot_r = jnp.sum(dta_r, axis=1, keepdims=True)
    ys = []
    for g in range(SSD_GROUPS):
        bm = xbc[:, SSD_W + g * ns:SSD_W + (g + 1) * ns]
        cm = xbc[:, SSD_W + SSD_GROUPS * ns + g * ns:SSD_W + SSD_GROUPS * ns + (g + 1) * ns]
        cmb = cm.astype(bf16)
        cb = lax.dot_general(cmb, bm.astype(bf16), (((1,), (1,)), ((), ())), preferred_element_type=f32)
        bm_t = bm.T
        for hg in range(per_g):
            h = g * per_g + hg
            hh = d * SSD_HEADS + h
            csc = cs_c[:, hh:hh + 1]
            csr = cs_r[hh:hh + 1, :]
            tot = tot_r[hh:hh + 1, :]
            decay = jnp.where(mask, jnp.exp(csc - csr), 0.0)
            xdt = (xbc[:, h * p:(h + 1) * p] * dt_c[:, hh:hh + 1]).astype(bf16)
            y_diag = jnp.dot((cb * decay).astype(bf16), xdt, preferred_element_type=f32)
            s_prev = st_ref[d, h]
            y_off = jnp.dot(cmb, s_prev.astype(bf16), preferred_element_type=f32) * jnp.exp(csc)
            bw_t = (bm_t * jnp.exp(tot - csr)).astype(bf16)
            st_ref[d, h] = jnp.exp(tot) * s_prev + jnp.dot(bw_t, xdt, preferred_element_type=f32)
            ys.append(y_diag + y_off)
    y_ref[0] = jnp.concatenate(ys, axis=1)


def _ssd_kernel(xf_ref, dcf_ref, drf_ref, xb_ref, dcb_ref, drb_ref, tri_ref, trit_ref,
                alog_l_ref, dtb_l_ref, alog_c_ref, dtb_c_ref, yf_ref, yb_ref, st_ref):
    @pl.when(pl.program_id(1) == 0)
    def _():
        st_ref[...] = jnp.zeros_like(st_ref)

    consts = (tri_ref, trit_ref, alog_l_ref, dtb_l_ref, alog_c_ref, dtb_c_ref)
    _ssd_dir(0, xf_ref, dcf_ref, drf_ref, *consts, yf_ref, st_ref)
    _ssd_dir(1, xb_ref, dcb_ref, drb_ref, *consts, yb_ref, st_ref)


def _ssd_scan(xbc, dt3, dt_t, a_log, dt_bias, l_lat):
    nb, t_all, _ = xbc.shape
    q = SSD_CHUNK
    lc = t_all - l_lat
    fwd, bwd = _seg_orders(l_lat // q, lc // q)
    low = jnp.tril(jnp.ones((q, q), f32))
    tri = jnp.stack([low, low.T])
    trit = jnp.stack([low.T, low])
    nh2 = 2 * SSD_HEADS
    alog_l = jnp.zeros((1, LANES), f32).at[0, :nh2].set(a_log.reshape(nh2))
    dtb_l = jnp.zeros((1, LANES), f32).at[0, :nh2].set(dt_bias.reshape(nh2))

    def specs(order):
        return [pl.BlockSpec((1, q, SSD_XBC), lambda b, i: (b, order(i), 0)),
                pl.BlockSpec((1, q, LANES), lambda b, i: (b, order(i), 0)),
                pl.BlockSpec((1, nh2, q), lambda b, i: (b, 0, order(i)))]

    def const(shape):
        return pl.BlockSpec(shape, lambda b, i: (0,) * len(shape))

    return pl.pallas_call(
        _ssd_kernel,
        out_shape=[jax.ShapeDtypeStruct((nb, t_all, SSD_W), f32)] * 2,
        grid=(nb, t_all // q),
        in_specs=specs(fwd) + specs(bwd) + [const((2, q, q)), const((2, q, q)), const((1, LANES)),
                                            const((1, LANES)), const((nh2, 1)), const((nh2, 1))],
        out_specs=[pl.BlockSpec((1, q, SSD_W), lambda b, i: (b, fwd(i), 0)),
                   pl.BlockSpec((1, q, SSD_W), lambda b, i: (b, bwd(i), 0))],
        scratch_shapes=[pltpu.VMEM((2, SSD_HEADS, SSD_STATE, SSD_HEAD_DIM), f32)],
        compiler_params=_params(("parallel", "arbitrary"), 32),
        name="ssd_scan",
    )(xbc, dt3, dt_t, xbc, dt3, dt_t, tri, trit, alog_l, dtb_l,
      a_log.reshape(nh2, 1), dt_bias.reshape(nh2, 1))


def _ssd_finish_kernel(yf_ref, yb_ref, xs_ref, z_ref, dsk_ref, g_ref, o_ref):
    y = yf_ref[...] + yb_ref[...] + dsk_ref[...] * xs_ref[...]
    y = y * _silu(z_ref[...].astype(f32))
    y = y * lax.rsqrt(jnp.mean(y * y, axis=-1, keepdims=True) + NORM_EPS) * g_ref[...]
    o_ref[...] = y.astype(o_ref.dtype)


def _ssd_finish(yf2, yb2, xbc2, proj2, d_skip, norm_g):
    r, w = yf2.shape
    tm = _pick(r, (1024, 768, 512, 384, 256, 128))
    row = lambda i: (i, 0)
    return pl.pallas_call(
        _ssd_finish_kernel,
        out_shape=jax.ShapeDtypeStruct((r, w), bf16),
        grid=(r // tm,),
        in_specs=[pl.BlockSpec((tm, w), row), pl.BlockSpec((tm, w), row), pl.BlockSpec((tm, w), row),
                  pl.BlockSpec((tm, w), lambda i: (i, C_SSD_Z // SSD_W)),
                  pl.BlockSpec((1, w), lambda i: (0, 0)), pl.BlockSpec((1, w), lambda i: (0, 0))],
        out_specs=pl.BlockSpec((tm, w), row),
        compiler_params=_params(("parallel",), 40),
        name="ssd_finish",
    )(yf2, yb2, xbc2, proj2, jnp.repeat(d_skip, SSD_HEAD_DIM).reshape(1, w), norm_g.reshape(1, w))


def _router_kernel(f_ref, w_ref, b_ref, idx_ref, wt_ref):
    hi = lax.Precision.HIGHEST
    logits = jnp.dot(f_ref[...], w_ref[...], precision=hi, preferred_element_type=f32) + b_ref[...]
    lane = lax.broadcasted_iota(jnp.int32, logits.shape, 1)
    neg = jnp.float32(-jnp.inf)
    logits = jnp.where(lane < N_EXPERTS, logits, neg)
    v1 = jnp.max(logits, axis=-1, keepdims=True)
    i1 = jnp.min(jnp.where(logits == v1, lane, GATE_LANES), axis=-1, keepdims=True)
    rest = jnp.where(lane == i1, neg, logits)
    v2 = jnp.max(rest, axis=-1, keepdims=True)
    i2 = jnp.min(jnp.where(rest == v2, lane, GATE_LANES), axis=-1, keepdims=True)
    e2 = jnp.exp(v2 - v1)
    w1 = 1.0 / (1.0 + e2)
    idx_ref[...] = jnp.where(lane == 0, i1, jnp.where(lane == 1, i2, 0))
    wt_ref[...] = jnp.where(lane == 0, w1, jnp.where(lane == 1, e2 * w1, 0.0))


def _router(fin, w_router, b_router):
    r, d = fin.shape
    tm = _pick(r, (512, 384, 256, 128))
    wp = jnp.zeros((d, GATE_LANES), f32).at[:, :N_EXPERTS].set(w_router)
    bp = jnp.zeros((1, GATE_LANES), f32).at[0, :N_EXPERTS].set(b_router)
    return pl.pallas_call(
        _router_kernel,
        out_shape=[jax.ShapeDtypeStruct((r, GATE_LANES), jnp.int32), jax.ShapeDtypeStruct((r, GATE_LANES), f32)],
        grid=(r // tm,),
        in_specs=[pl.BlockSpec((tm, d), lambda i: (i, 0)),
                  pl.BlockSpec((d, GATE_LANES), lambda i: (0, 0)),
                  pl.BlockSpec((1, GATE_LANES), lambda i: (0, 0))],
        out_specs=[pl.BlockSpec((tm, GATE_LANES), lambda i: (i, 0))] * 2,
        compiler_params=_params(("parallel",), 40),
        name="router",
    )(fin, wp, bp)


MOE_TG = 512


def _route_meta(e_idx, rg):
    n_assign = e_idx.size
    e_flat = e_idx.reshape(n_assign)
    onehot = (e_flat[:, None] == jnp.arange(N_EXPERTS, dtype=jnp.int32)[None, :]).astype(jnp.int32)
    csum = jnp.cumsum(onehot, axis=0)
    counts = csum[-1]
    rank = jnp.take_along_axis(csum, e_flat[:, None], axis=1)[:, 0] - 1
    gsize = ((counts + MOE_TG - 1) // MOE_TG) * MOE_TG
    ends = jnp.cumsum(gsize)
    pos = (ends - gsize)[e_flat] + rank
    tile_start = jnp.arange(rg // MOE_TG, dtype=jnp.int32) * MOE_TG
    tile_expert = jnp.minimum(jnp.sum((tile_start[:, None] >= ends[None, :]).astype(jnp.int32), axis=1),
                              N_EXPERTS - 1).astype(jnp.int32)
    n_used = (ends[-1:] // MOE_TG).astype(jnp.int32)
    src = jnp.zeros((rg,), jnp.int32).at[pos].set(jnp.arange(n_assign, dtype=jnp.int32) // TOP_K)
    return pos.reshape(e_idx.shape).astype(jnp.int32), src, tile_expert, n_used


def _row_gather_kernel(src_ref, x_hbm, o_ref, buf_ref, sem):
    rows = buf_ref.shape[0]

    def issue(r, carry):
        pltpu.make_async_copy(x_hbm.at[pl.ds(src_ref[r], 1), :], buf_ref.at[pl.ds(r, 1), :], sem).start()
        return carry

    lax.fori_loop(0, rows, issue, 0, unroll=8)
    pltpu.make_async_copy(x_hbm.at[pl.ds(0, rows), :], buf_ref, sem).wait()
    o_ref[...] = buf_ref[...].astype(o_ref.dtype)


def _row_gather(x2, src, out_dtype):
    rg = src.shape[0]
    d = x2.shape[1]
    return pl.pallas_call(
        _row_gather_kernel,
        out_shape=jax.ShapeDtypeStruct((rg, d), out_dtype),
        grid=(rg // MOE_TG,),
        in_specs=[pl.BlockSpec((MOE_TG,), lambda i: (i,), memory_space=pltpu.SMEM),
                  pl.BlockSpec(memory_space=pl.ANY)],
        out_specs=pl.BlockSpec((MOE_TG, d), lambda i: (i, 0)),
        scratch_shapes=[pltpu.VMEM((MOE_TG, d), x2.dtype), pltpu.SemaphoreType.DMA(())],
        compiler_params=_params(("arbitrary",), 32),
        name="moe_gather",
    )(src, x2)


def _cast_on_expert_change(te_ref, used, pairs):
    i = pl.program_id(1)
    changed = jnp.logical_or(i == 0, te_ref[i] != te_ref[jnp.maximum(i - 1, 0)])

    @pl.when(jnp.logical_and(used, changed))
    def _():
        for src, dst in pairs:
            dst[...] = src[0].astype(bf16)


def _gmm_swiglu_kernel(te_ref, nu_ref, a_ref, w1_ref, w3_ref, o_ref, w1b_ref, w3b_ref):
    used = pl.program_id(1) < nu_ref[0]
    _cast_on_expert_change(te_ref, used, ((w1_ref, w1b_ref), (w3_ref, w3b_ref)))

    @pl.when(used)
    def _():
        a = a_ref[...]
        g = jnp.dot(a, w1b_ref[...], preferred_element_type=f32)
        u = jnp.dot(a, w3b_ref[...], preferred_element_type=f32)
        o_ref[...] = (_silu(g) * u).astype(o_ref.dtype)

    @pl.when(jnp.logical_not(used))
    def _():
        o_ref[...] = jnp.zeros_like(o_ref)


def _gmm_out_kernel(te_ref, nu_ref, a_ref, w_ref, o_ref, wb_ref):
    used = pl.program_id(1) < nu_ref[0]
    _cast_on_expert_change(te_ref, used, ((w_ref, wb_ref),))

    @pl.when(used)
    def _():
        o_ref[...] = jnp.dot(a_ref[...], wb_ref[...], preferred_element_type=f32)

    @pl.when(jnp.logical_not(used))
    def _():
        o_ref[...] = jnp.zeros_like(o_ref)


def _used(i, nu):
    return jnp.minimum(i, nu[0] - 1)


def _gmm_swiglu(xs, w1, w3, tile_expert, n_used, tn=512):
    rg, k = xs.shape
    n = w1.shape[2]
    wspec = pl.BlockSpec((1, k, tn), lambda j, i, te, nu: (te[_used(i, nu)], 0, j))
    return pl.pallas_call(
        _gmm_swiglu_kernel,
        out_shape=jax.ShapeDtypeStruct((rg, n), bf16),
        grid_spec=pltpu.PrefetchScalarGridSpec(
            num_scalar_prefetch=2, grid=(n // tn, rg // MOE_TG),
            in_specs=[pl.BlockSpec((MOE_TG, k), lambda j, i, te, nu: (_used(i, nu), 0)), wspec, wspec],
            out_specs=pl.BlockSpec((MOE_TG, tn), lambda j, i, te, nu: (i, j)),
            scratch_shapes=[pltpu.VMEM((k, tn), bf16)] * 2),
        compiler_params=_params(("parallel", "arbitrary"), 48),
        name="moe_up",
    )(tile_expert, n_used, xs, w1, w3)


def _gmm_out(hmid, w2, tile_expert, n_used, tn=512):
    rg, k = hmid.shape
    n = w2.shape[2]
    return pl.pallas_call(
        _gmm_out_kernel,
        out_shape=jax.ShapeDtypeStruct((rg, n), f32),
        grid_spec=pltpu.PrefetchScalarGridSpec(
            num_scalar_prefetch=2, grid=(n // tn, rg // MOE_TG),
            in_specs=[pl.BlockSpec((MOE_TG, k), lambda j, i, te, nu: (_used(i, nu), 0)),
                      pl.BlockSpec((1, k, tn), lambda j, i, te, nu: (te[_used(i, nu)], 0, j))],
            out_specs=pl.BlockSpec((MOE_TG, tn), lambda j, i, te, nu: (i, j)),
            scratch_shapes=[pltpu.VMEM((k, tn), bf16)]),
        compiler_params=_params(("parallel", "arbitrary"), 48),
        name="moe_down",
    )(tile_expert, n_used, hmid, w2)


def _combine_kernel(p0_ref, p1_ref, ys_hbm, x_ref, m_ref, wt_ref, o_ref, b0_ref, b1_ref, sems, *, k_mod):
    rows = b0_ref.shape[0]

    def issue(r, carry):
        pltpu.make_async_copy(ys_hbm.at[pl.ds(p0_ref[r], 1), :], b0_ref.at[pl.ds(r, 1), :], sems.at[0]).start()
        pltpu.make_async_copy(ys_hbm.at[pl.ds(p1_ref[r], 1), :], b1_ref.at[pl.ds(r, 1), :], sems.at[1]).start()
        return carry

    lax.fori_loop(0, rows, issue, 0, unroll=8)
    pltpu.make_async_copy(ys_hbm.at[pl.ds(0, rows), :], b0_ref, sems.at[0]).wait()
    pltpu.make_async_copy(ys_hbm.at[pl.ds(0, rows), :], b1_ref, sems.at[1]).wait()
    mixed = wt_ref[:, 0:1] * b0_ref[...] + wt_ref[:, 1:2] * b1_ref[...]
    o_ref[...] = x_ref[...] + m_ref[0, k_mod:k_mod + 1, :] * mixed


def _moe_combine(ys, pos, wts, x2, modtab, nb, t_all, l_lat, k_mod):
    r, d = x2.shape
    tm = _pick(math.gcd(l_lat, t_all - l_lat), (256, 128))
    tpb = t_all // tm
    n_lat_tiles = l_lat // tm
    sspec = pl.BlockSpec((tm,), lambda i: (i,), memory_space=pltpu.SMEM)
    return pl.pallas_call(
        functools.partial(_combine_kernel, k_mod=k_mod),
        out_shape=jax.ShapeDtypeStruct((r, d), f32),
        grid=(r // tm,),
        in_specs=[sspec, sspec, pl.BlockSpec(memory_space=pl.ANY),
                  pl.BlockSpec((tm, d), lambda i: (i, 0)),
                  pl.BlockSpec((1, 6, d), lambda i: (jnp.where(i % tpb >= n_lat_tiles, nb, i // tpb), 0, 0)),
                  pl.BlockSpec((tm, GATE_LANES), lambda i: (i, 0))],
        out_specs=pl.BlockSpec((tm, d), lambda i: (i, 0)),
        scratch_shapes=[pltpu.VMEM((tm, d), f32), pltpu.VMEM((tm, d), f32), pltpu.SemaphoreType.DMA((2,))],
        input_output_aliases={3: 0},
        compiler_params=_params(("arbitrary",), 32),
        name="moe_combine",
    )(pos[:, 0], pos[:, 1], ys, x2, modtab, wts)


def _moe(fin32, xall, w_router, b_router, w1, w3, w2, modtab, nb, t_all, l_lat):
    rows = fin32.shape[0]
    rg = TOP_K * rows + N_EXPERTS * MOE_TG
    idx, wts = _router(fin32, w_router, b_router)
    pos, src, tile_expert, n_used = _route_meta(idx[:, :TOP_K], rg)
    xs = _row_gather(fin32, src, bf16)
    hmid = _gmm_swiglu(xs, w1, w3, tile_expert, n_used)
    ys = _gmm_out(hmid, w2, tile_expert, n_used)
    return _moe_combine(ys, pos, wts, xall, modtab, nb, t_all, l_lat, 5)


def _final_norm_kernel(x_ref, g_ref, o_ref):
    x = x_ref[0]
    o_ref[0] = x * lax.rsqrt(jnp.mean(x * x, axis=-1, keepdims=True) + NORM_EPS) * g_ref[...]


def _final_norm(x3, gain, l_lat):
    nb, t_all, d = x3.shape
    tr = _pick(math.gcd(l_lat, t_all), (512, 256, 128))
    return pl.pallas_call(
        _final_norm_kernel,
        out_shape=jax.ShapeDtypeStruct((nb, l_lat, d), f32),
        grid=(nb, l_lat // tr),
        in_specs=[pl.BlockSpec((1, tr, d), lambda b, j: (b, j, 0)),
                  pl.BlockSpec((1, d), lambda b, j: (0, 0))],
        out_specs=pl.BlockSpec((1, tr, d), lambda b, j: (b, j, 0)),
        compiler_params=_params(("parallel", "parallel"), 32),
        name="final_norm",
    )(x3, gain.reshape(1, d))


def _split_w_in(w):
    dt0 = C_SSD_XBC + SSD_XBC
    dt1 = dt0 + 2 * SSD_HEADS
    w_main = jnp.concatenate([w[:, :dt0], w[:, dt1:]], axis=1)
    w_dt = jnp.pad(w[:, dt0:dt1], ((0, 0), (0, LANES - (dt1 - dt0))))
    return w_main, w_dt


def _block_diag(w):
    h, bw, _ = w.shape
    eye = jnp.eye(h, dtype=w.dtype)
    return (eye[:, None, :, None] * w[:, :, None, :]).reshape(h * bw, h * bw)


def kernel(x, c, ctx, c_ctx, norm_mix, norm_ffn, ada_w, ada_b, w_in, hy_conv_w, hy_conv_b, hy_w1, hy_b1, hy_w2, hy_b2, hy_freq, hy_w3, hy_b3, hy_decay, hy_bias, lru_conv_w, lru_conv_b, lru_wa, lru_ba, lru_wx, lru_bx, lru_lambda, ssd_conv_w, ssd_conv_b, ssd_a_log, ssd_dt_bias, ssd_d, ssd_norm, att_q_norm, att_k_norm, w_branch, w_gate, b_gate, w_out, ffn_w1, ffn_w3, ffn_w2, moe_router, moe_router_b, moe_w1, moe_w3, moe_w2, norm_final):
    nb, l_lat, d = x.shape
    lc = ctx.shape[1]
    t_all = l_lat + lc
    rows = nb * t_all
    depth = norm_mix.shape[0]
    tm = _pick(t_all, (768, 384, 256, 128))

    mod_rows = -(-(nb + 1) // SUBLANES) * SUBLANES
    cvec = jnp.zeros((mod_rows, d), f32).at[:nb].set(c).at[nb].set(c_ctx)
    mods = _mods(cvec, ada_w, ada_b).reshape(depth, mod_rows, 6, d)

    cos2, sin2 = _rope_tables(l_lat, lc)
    dft_l, cs_l, nf_l = _dft_mats(l_lat)
    dft_c, cs_c, nf_c = _dft_mats(lc)

    xall = jnp.concatenate([x, ctx], axis=1).reshape(rows, d)

    for l in range(depth):
        with_ctx = l < depth - 1
        modtab = mods[l]
        h = _normmod(xall, norm_mix[l], modtab, nb, t_all, l_lat, 0, 1)
        w_main, w_dt = _split_w_in(w_in[l])
        proj2 = _mm(h, w_main, bf16, tm, 512)
        proj = proj2.reshape(nb, t_all, PROJ_W)
        dt3 = _mm(h, w_dt, f32, tm, LANES).reshape(nb, t_all, LANES)

        hy_args = (hy_w1[l], hy_b1[l], hy_w2[l], hy_b2[l], hy_freq[l], hy_w3[l], hy_b3[l], hy_decay[l], hy_bias[l])
        hu = _dwconv(proj, C_HY, HY_IN, hy_conv_w[l], hy_conv_b[l], (HY_SHORT - 1) // 2, l_lat)
        k_l = _hyena_kspec(l_lat, cs_l, nf_l, *hy_args)
        z_l = _hyconv(hu, (0, 0), hu, (0, 1), l_lat, dft_l, k_l[0])
        hy_l = _hyconv(z_l, (0, 0), hu, (0, 2), l_lat, dft_l, k_l[1])
        if with_ctx:
            k_c = _hyena_kspec(lc, cs_c, nf_c, *hy_args)
            z_c = _hyconv(hu, (l_lat // lc, 0), hu, (l_lat // lc, 1), lc, dft_c, k_c[0])
            hy_c = _hyconv(z_c, (0, 0), hu, (l_lat // lc, 2), lc, dft_c, k_c[1])
        else:
            hy_c = jnp.zeros((nb, lc, HY_W), bf16)
        y_hy = jnp.concatenate([hy_l, hy_c], axis=1).reshape(rows, HY_W)

        xc = _dwconv(proj, C_LRU_X, LRU_W, lru_conv_w[l], lru_conv_b[l], LRU_CONV // 2, l_lat)
        w_all = jnp.concatenate([_block_diag(lru_wa[l, 0]), _block_diag(lru_wx[l, 0]),
                                 _block_diag(lru_wa[l, 1]), _block_diag(lru_wx[l, 1])], axis=1).astype(bf16)
        bias_all = jnp.concatenate([lru_ba[l, 0].reshape(-1), lru_bx[l, 0].reshape(-1),
                                    lru_ba[l, 1].reshape(-1), lru_bx[l, 1].reshape(-1)]).reshape(1, 4 * LRU_W)
        a2, b2 = _lru_coefs(xc.reshape(rows, LRU_W), w_all, bias_all, lru_lambda[l])
        hf, hb = _lru_scan(a2.reshape(2, nb, t_all, LRU_W), b2.reshape(2, nb, t_all, LRU_W), l_lat)
        y_lru = _lru_finish(proj2, hf.reshape(rows, LRU_W), hb.reshape(rows, LRU_W))

        xbc = _dwconv(proj, C_SSD_XBC, SSD_XBC, ssd_conv_w[l], ssd_conv_b[l], SSD_CONV // 2, l_lat, act=True)
        dt_t = dt3[:, :, :2 * SSD_HEADS].transpose(0, 2, 1)
        yf, yb = _ssd_scan(xbc, dt3, dt_t, ssd_a_log[l], ssd_dt_bias[l], l_lat)
        y_ssd = _ssd_finish(yf.reshape(rows, SSD_W), yb.reshape(rows, SSD_W), xbc.reshape(rows, SSD_XBC),
                            proj2, ssd_d[l], ssd_norm[l])

        y_att = _attention(proj, cos2, sin2, att_q_norm[l], att_k_norm[l], l_lat).reshape(rows, ATT_Q)

        merged = _merge(h, [y_hy, y_lru, y_ssd, y_att], w_gate[l], b_gate[l], w_branch[l], tm)
        xall = _mm_resid(merged, w_out[l], xall, modtab, nb, t_all, l_lat, 2, tm, 1024)

        j = l // 2
        if l % 2 == 0:
            fin = _normmod(xall, norm_ffn[l], modtab, nb, t_all, l_lat, 3, 4)
            padc = FF_DENSE_PAD - FF_DENSE
            w1 = jnp.pad(ffn_w1[j], ((0, 0), (0, padc)))
            w3 = jnp.pad(ffn_w3[j], ((0, 0), (0, padc)))
            w2 = jnp.pad(ffn_w2[j], ((0, padc), (0, 0)))
            hmid = _mm_swiglu(fin, w1, w3, tm, 512)
            xall = _mm_resid(hmid, w2, xall, modtab, nb, t_all, l_lat, 5, _pick(t_all, (384, 128)), 512, vmem=56)
        else:
            fin32 = _normmod(xall, norm_ffn[l], modtab, nb, t_all, l_lat, 3, 4, out_dtype=f32)
            xall = _moe(fin32, xall, moe_router[j], moe_router_b[j], moe_w1[j], moe_w3[j], moe_w2[j],
                        modtab, nb, t_all, l_lat)

    return _final_norm(xall.reshape(nb, t_all, d), norm_final, l_lat)
```

```python
import functools
import math

import jax
import jax.numpy as jnp
from jax import lax
from jax.experimental import pallas as pl
from jax.experimental.pallas import tpu as pltpu

f32 = jnp.float32
bf16 = jnp.bfloat16

D_MODEL = 2048
DEPTH = 4
GRID_W = 64
NORM_EPS = 1e-6
HY_W = 512
HY_ORDER = 2
HY_SHORT = 3
HY_BANDS = 16
HY_IN = (HY_ORDER + 1) * HY_W
LRU_W = 512
LRU_HEADS = 8
LRU_BW = LRU_W // LRU_HEADS
LRU_CONV = 4
LRU_C = 8.0
SSD_HEADS = 8
SSD_HEAD_DIM = 64
SSD_W = SSD_HEADS * SSD_HEAD_DIM
SSD_GROUPS = 2
SSD_STATE = 128
SSD_CONV = 4
SSD_CHUNK = 128
SSD_XBC = SSD_W + 2 * SSD_GROUPS * SSD_STATE
ATT_HEADS = 8
ATT_KV_HEADS = 2
ATT_HEAD_DIM = 128
ROPE_THETA = 10000.0
ATT_Q = ATT_HEADS * ATT_HEAD_DIM
ATT_KV = ATT_KV_HEADS * ATT_HEAD_DIM
FF_DENSE = 5504
N_EXPERTS = 8
TOP_K = 2

LANES = 128
SUBLANES = 8
MIB = 2 ** 20

C_HY = 0
C_LRU_G = C_HY + HY_IN
C_LRU_X = C_LRU_G + LRU_W
C_SSD_Z = C_LRU_X + LRU_W
C_SSD_XBC = C_SSD_Z + SSD_W
C_Q = C_SSD_XBC + SSD_XBC
C_K = C_Q + ATT_Q
C_V = C_K + ATT_KV
PROJ_W = C_V + ATT_KV
FF_DENSE_PAD = 5632
GATE_LANES = LANES


def _params(sem, vmem_mib):
    return pltpu.CompilerParams(dimension_semantics=sem, vmem_limit_bytes=vmem_mib * MIB)


def _pick(n, cands):
    for c in cands:
        if n % c == 0:
            return c
    raise ValueError(f"no tile for {n} in {cands}")


def _silu(v):
    return v * jax.nn.sigmoid(v)


def _softplus(v):
    return jnp.maximum(v, 0.0) + jnp.log1p(jnp.exp(-jnp.abs(v)))


def _mods_kernel(c_ref, w_ref, b_ref, o_ref):
    s = _silu(c_ref[...]).astype(bf16)
    o_ref[0] = jnp.dot(s, w_ref[0].astype(bf16), preferred_element_type=f32) + b_ref[0]


def _mods(cvec, ada_w, ada_b):
    depth, d, n = ada_w.shape
    rows = cvec.shape[0]
    tn = 1024
    return pl.pallas_call(
        _mods_kernel,
        out_shape=jax.ShapeDtypeStruct((depth, rows, n), f32),
        grid=(depth, n // tn),
        in_specs=[pl.BlockSpec((rows, d), lambda l, j: (0, 0)),
                  pl.BlockSpec((1, d, tn), lambda l, j: (l, 0, j)),
                  pl.BlockSpec((1, 1, tn), lambda l, j: (l, 0, j))],
        out_specs=pl.BlockSpec((1, rows, tn), lambda l, j: (l, 0, j)),
        compiler_params=_params(("parallel", "parallel"), 40),
        name="mods",
    )(cvec, ada_w, ada_b.reshape(depth, 1, n))


def _row_mod(ml_ref, mc_ref, k, is_ctx):
    return jnp.where(is_ctx, mc_ref[0, k:k + 1, :], ml_ref[0, k:k + 1, :])


def _is_ctx_rows(tm, tpb, l_lat, axis):
    t = (pl.program_id(axis) % tpb) * tm + lax.broadcasted_iota(jnp.int32, (tm, 1), 0)
    return t >= l_lat


def _normmod_kernel(x_ref, g_ref, ml_ref, mc_ref, o_ref, *, tm, tpb, l_lat, k_shift, k_scale):
    x = x_ref[...]
    y = x * lax.rsqrt(jnp.mean(x * x, axis=-1, keepdims=True) + NORM_EPS) * g_ref[...]
    is_ctx = _is_ctx_rows(tm, tpb, l_lat, 0)
    shift = _row_mod(ml_ref, mc_ref, k_shift, is_ctx)
    scale = _row_mod(ml_ref, mc_ref, k_scale, is_ctx)
    o_ref[...] = (y * (1.0 + scale) + shift).astype(o_ref.dtype)


def _normmod(x2, gain, modtab, nb, t_all, l_lat, k_shift, k_scale, out_dtype=bf16):
    r, d = x2.shape
    tm = _pick(t_all, (768, 384, 256, 128))
    tpb = t_all // tm
    return pl.pallas_call(
        functools.partial(_normmod_kernel, tm=tm, tpb=tpb, l_lat=l_lat, k_shift=k_shift, k_scale=k_scale),
        out_shape=jax.ShapeDtypeStruct((r, d), out_dtype),
        grid=(r // tm,),
        in_specs=[pl.BlockSpec((tm, d), lambda i: (i, 0)),
                  pl.BlockSpec((1, d), lambda i: (0, 0)),
                  pl.BlockSpec((1, 6, d), lambda i: (i // tpb, 0, 0)),
                  pl.BlockSpec((1, 6, d), lambda i: (nb, 0, 0))],
        out_specs=pl.BlockSpec((tm, d), lambda i: (i, 0)),
        compiler_params=_params(("parallel",), 56),
        name="normmod",
    )(x2, gain.reshape(1, d), modtab, modtab)


def _cast_on_first_row_tile(pairs):
    @pl.when(pl.program_id(1) == 0)
    def _():
        for src, dst in pairs:
            dst[...] = src[...].astype(bf16)


def _mm_kernel(a_ref, w_ref, o_ref, wb_ref):
    _cast_on_first_row_tile(((w_ref, wb_ref),))
    o_ref[...] = jnp.dot(a_ref[...], wb_ref[...], preferred_element_type=f32).astype(o_ref.dtype)


def _wspec(w, lead, tn):
    k = w.shape[-2]
    nl = len(lead)
    return pl.BlockSpec((None,) * nl + (k, tn), lambda j, i: tuple(lead) + (0, j), pipeline_mode=pl.Buffered(1))


def _mm3_kernel(a_ref, w_ref, o_ref):
    a = a_ref[...]
    w = w_ref[...]
    ah = a.astype(bf16)
    al = (a - ah.astype(f32)).astype(bf16)
    wh = w.astype(bf16)
    wl = (w - wh.astype(f32)).astype(bf16)
    o_ref[...] = (jnp.dot(ah, wh, preferred_element_type=f32) + jnp.dot(ah, wl, preferred_element_type=f32)
                  + jnp.dot(al, wh, preferred_element_type=f32))


def _mm3(a, w):
    m, k = a.shape
    n = w.shape[1]
    tm = _pick(m, (256, 128))
    return pl.pallas_call(
        _mm3_kernel,
        out_shape=jax.ShapeDtypeStruct((m, n), f32),
        grid=(m // tm,),
        in_specs=[pl.BlockSpec((tm, k), lambda i: (i, 0)),
                  pl.BlockSpec((k, n), lambda i: (0, 0))],
        out_specs=pl.BlockSpec((tm, n), lambda i: (i, 0)),
        compiler_params=_params(("parallel",), 48),
        name="mm3",
    )(a, w)


def _mm(a, w, out_dtype, tm, tn, vmem=48):
    m, k = a.shape
    n = w.shape[1]
    return pl.pallas_call(
        _mm_kernel,
        out_shape=jax.ShapeDtypeStruct((m, n), out_dtype),
        grid=(n // tn, m // tm),
        in_specs=[pl.BlockSpec((tm, k), lambda j, i: (i, 0)),
                  _wspec(w, (), tn)],
        out_specs=pl.BlockSpec((tm, tn), lambda j, i: (i, j)),
        scratch_shapes=[pltpu.VMEM((k, tn), bf16)],
        compiler_params=_params(("parallel", "arbitrary"), vmem),
        name="mm",
    )(a, w)


def _mm_swiglu_kernel(a_ref, w1_ref, w3_ref, o_ref, w1b_ref, w3b_ref):
    _cast_on_first_row_tile(((w1_ref, w1b_ref), (w3_ref, w3b_ref)))
    a = a_ref[...]
    g = jnp.dot(a, w1b_ref[...], preferred_element_type=f32)
    u = jnp.dot(a, w3b_ref[...], preferred_element_type=f32)
    o_ref[...] = (_silu(g) * u).astype(o_ref.dtype)


def _mm_swiglu(a, w1, w3, tm, tn, vmem=48):
    m, k = a.shape
    n = w1.shape[1]
    wspec = _wspec(w1, (), tn)
    return pl.pallas_call(
        _mm_swiglu_kernel,
        out_shape=jax.ShapeDtypeStruct((m, n), bf16),
        grid=(n // tn, m // tm),
        in_specs=[pl.BlockSpec((tm, k), lambda j, i: (i, 0)), wspec, wspec],
        out_specs=pl.BlockSpec((tm, tn), lambda j, i: (i, j)),
        scratch_shapes=[pltpu.VMEM((k, tn), bf16)] * 2,
        compiler_params=_params(("parallel", "arbitrary"), vmem),
        name="mm_swiglu",
    )(a, w1, w3)


def _mm_resid_kernel(a_ref, w_ref, x_ref, ml_ref, mc_ref, o_ref, wb_ref, *, tm, tpb, l_lat, k_mod):
    _cast_on_first_row_tile(((w_ref, wb_ref),))
    acc = jnp.dot(a_ref[...], wb_ref[...], preferred_element_type=f32)
    mod = _row_mod(ml_ref, mc_ref, k_mod, _is_ctx_rows(tm, tpb, l_lat, 1))
    o_ref[...] = x_ref[...] + mod * acc


def _mm_resid(a, w, x2, modtab, nb, t_all, l_lat, k_mod, tm, tn, lead=(), vmem=48):
    m, k = a.shape
    n = w.shape[-1]
    tpb = t_all // tm
    return pl.pallas_call(
        functools.partial(_mm_resid_kernel, tm=tm, tpb=tpb, l_lat=l_lat, k_mod=k_mod),
        out_shape=jax.ShapeDtypeStruct((m, n), f32),
        grid=(n // tn, m // tm),
        in_specs=[pl.BlockSpec((tm, k), lambda j, i: (i, 0)),
                  _wspec(w, lead, tn),
                  pl.BlockSpec((tm, tn), lambda j, i: (i, j)),
                  pl.BlockSpec((1, 6, tn), lambda j, i: (i // tpb, 0, j)),
                  pl.BlockSpec((1, 6, tn), lambda j, i: (nb, 0, j))],
        out_specs=pl.BlockSpec((tm, tn), lambda j, i: (i, j)),
        scratch_shapes=[pltpu.VMEM((k, tn), bf16)],
        input_output_aliases={2: 0},
        compiler_params=_params(("parallel", "arbitrary"), vmem),
        name="mm_resid",
    )(a, w, x2, modtab, modtab)


def _merge_kernel(h_ref, y0_ref, y1_ref, y2_ref, y3_ref, wg_ref, bg_ref, wb_ref, o_ref, wgb_ref, wbb_ref,
                  *, offs):
    _cast_on_first_row_tile(((wg_ref, wgb_ref), (wb_ref, wbb_ref)))
    h = h_ref[...]
    acc = None
    for i, y_ref in enumerate((y0_ref, y1_ref, y2_ref, y3_ref)):
        g = jnp.dot(h, wgb_ref[i], preferred_element_type=f32) + bg_ref[i]
        p = jnp.dot(y_ref[...], wbb_ref[offs[i]:offs[i + 1], :], preferred_element_type=f32)
        term = jax.nn.sigmoid(g) * p
        acc = term if acc is None else acc + term
    o_ref[...] = acc.astype(o_ref.dtype)


def _merge(h, ys, wg_all, bg, wb_all, layer, tm, tn=256):
    m, d = h.shape
    n = wg_all.shape[-1]
    kb = wb_all.shape[-2]
    offs = [0]
    for y in ys:
        offs.append(offs[-1] + y.shape[1])
    one = pl.Buffered(1)
    in_specs = [pl.BlockSpec((tm, d), lambda j, i: (i, 0))]
    in_specs += [pl.BlockSpec((tm, y.shape[1]), lambda j, i: (i, 0)) for y in ys]
    in_specs += [pl.BlockSpec((None, 4, d, tn), lambda j, i: (layer, 0, 0, j), pipeline_mode=one),
                 pl.BlockSpec((4, 1, tn), lambda j, i: (0, 0, j)),
                 pl.BlockSpec((None, kb, tn), lambda j, i: (layer, 0, j), pipeline_mode=one)]
    return pl.pallas_call(
        functools.partial(_merge_kernel, offs=tuple(offs)),
        out_shape=jax.ShapeDtypeStruct((m, n), bf16),
        grid=(n // tn, m // tm),
        in_specs=in_specs,
        out_specs=pl.BlockSpec((tm, tn), lambda j, i: (i, j)),
        scratch_shapes=[pltpu.VMEM((4, d, tn), bf16), pltpu.VMEM((kb, tn), bf16)],
        compiler_params=_params(("parallel", "arbitrary"), 56),
        name="merge",
    )(h, *ys, wg_all, bg.reshape(4, 1, n), wb_all)


def _norm_rope(x, gain, cos, sin):
    y = x * lax.rsqrt(jnp.mean(x * x, axis=-1, keepdims=True) + NORM_EPS) * gain
    lane = lax.broadcasted_iota(jnp.int32, y.shape, 1)
    partner = jnp.where(lane % 2 == 0, pltpu.roll(y, ATT_HEAD_DIM - 1, 1), pltpu.roll(y, 1, 1))
    return y * cos + partner * sin


def _attn_kernel(q_ref, k_ref, v_ref, cos_ref, sin_ref, qg_ref, kg_ref, o_ref, ks_ref, vs_ref,
                 *, tq, l_lat, n_lat_blocks):
    qi = pl.program_id(2)
    grp = ATT_HEADS // ATT_KV_HEADS
    hd = ATT_HEAD_DIM

    @pl.when(qi == 0)
    def _():
        ks_ref[...] = _norm_rope(k_ref[0].astype(f32), kg_ref[...], cos_ref[...], sin_ref[...]).astype(bf16)
        vs_ref[:, :hd] = v_ref[0].astype(bf16)
        vs_ref[:, hd:] = jnp.ones((vs_ref.shape[0], hd), bf16)

    r0 = pl.multiple_of(qi * tq, tq)
    cos = cos_ref[pl.ds(r0, tq), :]
    sin = sin_ref[pl.ds(r0, tq), :]
    q = q_ref[0].astype(f32)
    sc = (hd ** -0.5) * math.log2(math.e)
    qhs = [(_norm_rope(q[:, h * hd:(h + 1) * hd], qg_ref[...], cos, sin) * sc).astype(bf16) for h in range(grp)]

    def attend(ks, vs):
        for h in range(grp):
            s = lax.dot_general(qhs[h], ks, (((1,), (1,)), ((), ())), preferred_element_type=f32)
            p = jnp.exp2(s - jnp.max(s, axis=-1, keepdims=True)).astype(bf16)
            oe = jnp.dot(p, vs, preferred_element_type=f32)
            o_ref[0, :, h * hd:(h + 1) * hd] = (oe[:, :hd] / oe[:, hd:hd + 1]).astype(o_ref.dtype)

    @pl.when(qi < n_lat_blocks)
    def _():
        attend(ks_ref[...], vs_ref[...])

    @pl.when(qi >= n_lat_blocks)
    def _():
        attend(ks_ref[l_lat:, :], vs_ref[l_lat:, :])


def _attention(proj, cos2, sin2, q_gain, k_gain, l_lat):
    nb, t_all, _ = proj.shape
    lc = t_all - l_lat
    tq = lc if (l_lat % lc == 0 and lc <= 256) else _pick(math.gcd(l_lat, lc), (256, 128))
    grp = ATT_HEADS // ATT_KV_HEADS
    qw = grp * ATT_HEAD_DIM
    hd = ATT_HEAD_DIM
    return pl.pallas_call(
        functools.partial(_attn_kernel, tq=tq, l_lat=l_lat, n_lat_blocks=l_lat // tq),
        out_shape=jax.ShapeDtypeStruct((nb, t_all, ATT_Q), bf16),
        grid=(nb, ATT_KV_HEADS, t_all // tq),
        in_specs=[pl.BlockSpec((1, tq, qw), lambda b, g, i: (b, i, C_Q // qw + g)),
                  pl.BlockSpec((1, t_all, hd), lambda b, g, i: (b, 0, C_K // hd + g)),
                  pl.BlockSpec((1, t_all, hd), lambda b, g, i: (b, 0, C_V // hd + g)),
                  pl.BlockSpec((t_all, hd), lambda b, g, i: (0, 0)),
                  pl.BlockSpec((t_all, hd), lambda b, g, i: (0, 0)),
                  pl.BlockSpec((1, hd), lambda b, g, i: (0, 0)),
                  pl.BlockSpec((1, hd), lambda b, g, i: (0, 0))],
        out_specs=pl.BlockSpec((1, tq, qw), lambda b, g, i: (b, i, g)),
        scratch_shapes=[pltpu.VMEM((t_all, hd), bf16), pltpu.VMEM((t_all, 2 * hd), bf16)],
        compiler_params=_params(("parallel", "parallel", "arbitrary"), 56),
        name="attention",
    )(proj, proj, proj, cos2, sin2, q_gain.reshape(1, hd), k_gain.reshape(1, hd))


def _rope_tables(l_lat, lc):
    axis_dim = ATT_HEAD_DIM // 2
    rows = l_lat // GRID_W
    row_id = jnp.repeat(jnp.arange(rows, dtype=f32), GRID_W)
    col_id = jnp.tile(jnp.arange(GRID_W, dtype=f32), rows)
    inv = ROPE_THETA ** (-jnp.arange(0, axis_dim, 2, dtype=f32) / axis_dim)
    ang = jnp.concatenate([row_id[:, None] * inv, col_id[:, None] * inv], axis=-1)
    cos = jnp.repeat(jnp.cos(ang), 2, axis=-1)
    sin = jnp.repeat(jnp.sin(ang), 2, axis=-1)
    sign = jnp.tile(jnp.array([-1.0, 1.0], f32), ATT_HEAD_DIM // 2)
    cos2 = jnp.concatenate([cos, jnp.ones((lc, ATT_HEAD_DIM), f32)], axis=0)
    sin2 = jnp.concatenate([sin * sign, jnp.zeros((lc, ATT_HEAD_DIM), f32)], axis=0)
    return cos2, sin2


def _dwconv_kernel(u_ref, w_ref, b_ref, o_ref, *, ksz, pad_left, l_lat, t_all, act):
    u = u_ref[0].astype(f32)
    t = lax.broadcasted_iota(jnp.int32, (t_all, 1), 0)
    is_ctx = t >= l_lat
    t_loc = jnp.where(is_ctx, t - l_lat, t)
    seg_len = jnp.where(is_ctx, t_all - l_lat, l_lat)
    acc = None
    for k in range(ksz):
        s = k - pad_left
        if s == 0:
            term = u
        else:
            shifted = pltpu.roll(u, (t_all - s) % t_all, 0)
            term = jnp.where((t_loc + s >= 0) & (t_loc + s < seg_len), shifted, 0.0)
        term = term * w_ref[k:k + 1, :]
        acc = term if acc is None else acc + term
    acc = acc + b_ref[...]
    if act:
        acc = _silu(acc)
    o_ref[0] = acc.astype(o_ref.dtype)


def _dwconv(proj, col0, width, w, b, pad_left, l_lat, act=False):
    nb, t_all, _ = proj.shape
    ksz = w.shape[0]
    cw = 256
    return pl.pallas_call(
        functools.partial(_dwconv_kernel, ksz=ksz, pad_left=pad_left, l_lat=l_lat, t_all=t_all, act=act),
        out_shape=jax.ShapeDtypeStruct((nb, t_all, width), f32),
        grid=(nb, width // cw),
        in_specs=[pl.BlockSpec((1, t_all, cw), lambda bi, c: (bi, 0, col0 // cw + c)),
                  pl.BlockSpec((ksz, cw), lambda bi, c: (0, c)),
                  pl.BlockSpec((1, cw), lambda bi, c: (0, c))],
        out_specs=pl.BlockSpec((1, t_all, cw), lambda bi, c: (bi, 0, c)),
        compiler_params=_params(("parallel", "parallel"), 48),
        name="dwconv",
    )(proj, w, b.reshape(1, width))


def _hyconv_kernel(u_ref, g_ref, m_ref, mi_ref, k_ref, o_ref, ub_ref, acc_ref, *, fc, nf):
    f = pl.program_id(2)

    @pl.when(f == 0)
    def _():
        ub_ref[...] = u_ref[0].astype(bf16)
        acc_ref[...] = jnp.zeros_like(acc_ref)

    spec = jnp.dot(m_ref[0], ub_ref[...], preferred_element_type=f32)
    a, bv = spec[:fc], spec[fc:]
    kk = k_ref[0]
    kr, ki = kk[:fc], kk[fc:]
    y = jnp.concatenate([a * kr + bv * ki, a * ki - bv * kr], axis=0).astype(bf16)
    acc_ref[...] += jnp.dot(mi_ref[0], y, preferred_element_type=f32)

    @pl.when(f == nf - 1)
    def _():
        o_ref[0] = (acc_ref[...] * g_ref[0]).astype(o_ref.dtype)


def _hyconv(u, u_blk, gate, gate_blk, lseg, dft, kspec):
    m_mat, mi_mat = dft
    nf, fc2, _ = m_mat.shape
    fc = fc2 // 2
    nb = u.shape[0]
    cw = HY_W
    return pl.pallas_call(
        functools.partial(_hyconv_kernel, fc=fc, nf=nf),
        out_shape=jax.ShapeDtypeStruct((nb, lseg, cw), bf16),
        grid=(nb, 1, nf),
        in_specs=[pl.BlockSpec((1, lseg, cw), lambda b, c, f: (b,) + u_blk),
                  pl.BlockSpec((1, lseg, cw), lambda b, c, f: (b,) + gate_blk),
                  pl.BlockSpec((1, fc2, lseg), lambda b, c, f: (f, 0, 0)),
                  pl.BlockSpec((1, lseg, fc2), lambda b, c, f: (f, 0, 0)),
                  pl.BlockSpec((1, fc2, cw), lambda b, c, f: (f, 0, 0))],
        out_specs=pl.BlockSpec((1, lseg, cw), lambda b, c, f: (b, 0, 0)),
        scratch_shapes=[pltpu.VMEM((lseg, cw), bf16), pltpu.VMEM((lseg, cw), f32)],
        compiler_params=_params(("parallel", "arbitrary", "arbitrary"), 52),
        name="hyconv",
    )(u, gate, m_mat, mi_mat, kspec)


def _dft_cs(lseg):
    n = 2 * lseg
    f = jnp.arange(lseg, dtype=jnp.int32)[:, None]
    s = jnp.arange(lseg, dtype=jnp.int32)[None, :]
    ph = ((2 * f + 1) * s) % (2 * n)
    ang = ph.astype(f32) * (math.pi / n)
    return jnp.cos(ang), jnp.sin(ang)


def _dft_mats(lseg):
    fc = min(256, lseg // 2)
    nf = lseg // fc
    c, s = _dft_cs(lseg)
    c3 = c.reshape(nf, fc, lseg)
    s3 = s.reshape(nf, fc, lseg)
    m_mat = jnp.concatenate([c3, s3], axis=1).astype(bf16)
    scale = 1.0 / lseg
    mi_mat = (jnp.concatenate([c3, -s3], axis=1) * scale).transpose(0, 2, 1).astype(bf16)
    return (m_mat, mi_mat), (c, s), (nf, fc)


def _hyena_filters(lseg, w1, b1, w2, b2, freq, w3, b3, decay):
    hi = lax.Precision.HIGHEST
    t = jnp.arange(lseg, dtype=f32)
    tn = t / lseg
    bands = jnp.linspace(1e-4, HY_BANDS - 1, HY_BANDS, dtype=f32)
    ang = (2.0 * math.pi / lseg) * t[:, None] * bands[None, :]
    feat = jnp.concatenate([tn[:, None], jnp.cos(ang), -jnp.sin(ang)], axis=-1)
    hdn = jnp.sin(freq[0] * (jnp.dot(feat, w1, precision=hi) + b1))
    hdn = jnp.sin(freq[1] * (jnp.dot(hdn, w2, precision=hi) + b2))
    filt = (jnp.dot(hdn, w3, precision=hi) + b3).reshape(lseg, HY_ORDER, 2, HY_W)
    window = jnp.exp(-tn[:, None, None, None] * jnp.abs(decay)[None])
    return filt * window


def _hyena_kspec(lseg, cs, nf_fc, w1, b1, w2, b2, freq, w3, b3, decay, bias):
    c, s = cs
    nf, fc = nf_fc
    filt = _hyena_filters(lseg, w1, b1, w2, b2, freq, w3, b3, decay)
    out = []
    for o in range(HY_ORDER):
        hf = filt[:, o, 0].at[0].add(bias[o])
        hb = filt[:, o, 1].at[0].set(0.0)
        kr = _mm3(c, hf + hb).reshape(nf, fc, HY_W)
        ki = _mm3(s, hb - hf).reshape(nf, fc, HY_W)
        out.append(jnp.concatenate([kr, ki], axis=1))
    return out


def _lru_coef_kernel(xc_ref, w_ref, bias_ref, lam_ref, a_ref, b_ref):
    xc = xc_ref[...]
    z = jnp.dot(xc.astype(bf16), w_ref[...], preferred_element_type=f32) + bias_ref[...]
    g = jax.nn.sigmoid(z)
    for d in range(2):
        r = g[:, d * 2 * LRU_W:d * 2 * LRU_W + LRU_W]
        i = g[:, d * 2 * LRU_W + LRU_W:(d + 1) * 2 * LRU_W]
        log_a = -LRU_C * r * _softplus(-lam_ref[d:d + 1, :])
        th = jnp.tanh(log_a)
        one_minus_a2 = -2.0 * th / (1.0 - th)
        a_ref[d] = jnp.exp(log_a)
        b_ref[d] = jnp.sqrt(one_minus_a2) * i * xc


def _lru_coefs(xc2, w_all, bias_all, lam):
    r, w = xc2.shape
    tm = _pick(r, (512, 384, 256, 128))
    return pl.pallas_call(
        _lru_coef_kernel,
        out_shape=[jax.ShapeDtypeStruct((2, r, w), f32)] * 2,
        grid=(r // tm,),
        in_specs=[pl.BlockSpec((tm, w), lambda i: (i, 0)),
                  pl.BlockSpec((w, 4 * w), lambda i: (0, 0)),
                  pl.BlockSpec((1, 4 * w), lambda i: (0, 0)),
                  pl.BlockSpec((2, w), lambda i: (0, 0))],
        out_specs=[pl.BlockSpec((2, tm, w), lambda i: (0, i, 0))] * 2,
        compiler_params=_params(("parallel",), 40),
        name="lru_coefs",
    )(xc2, w_all, bias_all, lam)


def _lru_scan_kernel(af_ref, bf_ref, ab_ref, bb_ref, hf_ref, hb_ref, st_ref, *, blk):
    @pl.when(pl.program_id(1) == 0)
    def _():
        st_ref[...] = jnp.zeros_like(st_ref)

    def body(g, carry):
        hf, hb = carry
        base = pl.multiple_of(g * SUBLANES, SUBLANES)
        rbase = pl.multiple_of(blk - SUBLANES - g * SUBLANES, SUBLANES)
        for r in range(SUBLANES):
            hf = af_ref[0, 0, pl.ds(base + r, 1), :] * hf + bf_ref[0, 0, pl.ds(base + r, 1), :]
            hf_ref[0, pl.ds(base + r, 1), :] = hf
            rr = rbase + (SUBLANES - 1 - r)
            hb = ab_ref[0, 0, pl.ds(rr, 1), :] * hb + bb_ref[0, 0, pl.ds(rr, 1), :]
            hb_ref[0, pl.ds(rr, 1), :] = hb
        return hf, hb

    hf, hb = lax.fori_loop(0, blk // SUBLANES, body, (st_ref[0:1, :], st_ref[1:2, :]))
    st_ref[0:1, :] = hf
    st_ref[1:2, :] = hb


def _seg_orders(n_lat, n_ctx):
    def fwd(i):
        return jnp.where(i < n_ctx, n_lat + i, i - n_ctx)

    def bwd(i):
        return jnp.where(i < n_ctx, n_lat + n_ctx - 1 - i, n_lat - 1 - (i - n_ctx))

    return fwd, bwd


def _lru_scan(a4, b4, l_lat):
    _, nb, t_all, w = a4.shape
    lc = t_all - l_lat
    blk = _pick(math.gcd(l_lat, lc), (256, 128))
    fwd, bwd = _seg_orders(l_lat // blk, lc // blk)
    spec_f = pl.BlockSpec((1, 1, blk, w), lambda b, i: (0, b, fwd(i), 0))
    spec_b = pl.BlockSpec((1, 1, blk, w), lambda b, i: (1, b, bwd(i), 0))
    return pl.pallas_call(
        functools.partial(_lru_scan_kernel, blk=blk),
        out_shape=[jax.ShapeDtypeStruct((nb, t_all, w), f32)] * 2,
        grid=(nb, t_all // blk),
        in_specs=[spec_f, spec_f, spec_b, spec_b],
        out_specs=[pl.BlockSpec((1, blk, w), lambda b, i: (b, fwd(i), 0)),
                   pl.BlockSpec((1, blk, w), lambda b, i: (b, bwd(i), 0))],
        scratch_shapes=[pltpu.VMEM((SUBLANES, w), f32)],
        compiler_params=_params(("parallel", "arbitrary"), 32),
        name="lru_scan",
    )(a4, b4, a4, b4)


def _lru_finish_kernel(g_ref, hf_ref, hb_ref, o_ref):
    g = g_ref[...].astype(f32)
    gelu = 0.5 * g * (1.0 + jnp.tanh(math.sqrt(2.0 / math.pi) * (g + 0.044715 * (g * g * g))))
    o_ref[...] = (gelu * (hf_ref[...] + hb_ref[...])).astype(o_ref.dtype)


def _lru_finish(proj2, hf2, hb2):
    r, w = hf2.shape
    tm = _pick(r, (1024, 768, 512, 384, 256, 128))
    return pl.pallas_call(
        _lru_finish_kernel,
        out_shape=jax.ShapeDtypeStruct((r, w), bf16),
        grid=(r // tm,),
        in_specs=[pl.BlockSpec((tm, w), lambda i: (i, C_LRU_G // LRU_W)),
                  pl.BlockSpec((tm, w), lambda i: (i, 0)),
                  pl.BlockSpec((tm, w), lambda i: (i, 0))],
        out_specs=pl.BlockSpec((tm, w), lambda i: (i, 0)),
        compiler_params=_params(("parallel",), 32),
        name="lru_finish",
    )(proj2, hf2, hb2)


def _ssd_dir(d, x_ref, dc_ref, dr_ref, tri_ref, trit_ref, alog_l_ref, dtb_l_ref, alog_c_ref, dtb_c_ref,
             y_ref, st_ref):
    hi = lax.Precision.HIGHEST
    q = SSD_CHUNK
    p = SSD_HEAD_DIM
    ns = SSD_STATE
    per_g = SSD_HEADS // SSD_GROUPS
    xbc = x_ref[0]
    tri = tri_ref[d]
    mask = tri > 0.5
    dt_c = _softplus(dc_ref[0] + dtb_l_ref[...])
    dta_c = dt_c * (-jnp.exp(alog_l_ref[...]))
    cs_c = jnp.dot(tri, dta_c, precision=hi, preferred_element_type=f32)
    dt_r = _softplus(dr_ref[0] + dtb_c_ref[...])
    dta_r = dt_r * (-jnp.exp(alog_c_ref[...]))
    cs_r = jnp.dot(dta_r, trit_ref[d], precision=hi, preferred_element_type=f32)
    tot_r = jnp.sum(dta_r, axis=1, keepdims=True)
    ys = []
    for g in range(SSD_GROUPS):
        bm = xbc[:, SSD_W + g * ns:SSD_W + (g + 1) * ns]
        cm = xbc[:, SSD_W + SSD_GROUPS * ns + g * ns:SSD_W + SSD_GROUPS * ns + (g + 1) * ns]
        cmb = cm.astype(bf16)
        cb = lax.dot_general(cmb, bm.astype(bf16), (((1,), (1,)), ((), ())), preferred_element_type=f32)
        bm_t = bm.T
        for hg in range(per_g):
            h = g * per_g + hg
            hh = d * SSD_HEADS + h
            csc = cs_c[:, hh:hh + 1]
            csr = cs_r[hh:hh + 1, :]
            tot = tot_r[hh:hh + 1, :]
            decay = jnp.where(mask, jnp.exp(csc - csr), 0.0)
            xdt = (xbc[:, h * p:(h + 1) * p] * dt_c[:, hh:hh + 1]).astype(bf16)
            y_diag = jnp.dot((cb * decay).astype(bf16), xdt, preferred_element_type=f32)
            s_prev = st_ref[d, h]
            y_off = jnp.dot(cmb, s_prev.astype(bf16), preferred_element_type=f32) * jnp.exp(csc)
            bw_t = (bm_t * jnp.exp(tot - csr)).astype(bf16)
            st_ref[d, h] = jnp.exp(tot) * s_prev + jnp.dot(bw_t, xdt, preferred_element_type=f32)
            ys.append(y_diag + y_off)
    y_ref[0] = jnp.concatenate(ys, axis=1)


def _ssd_kernel(xf_ref, dcf_ref, drf_ref, xb_ref, dcb_ref, drb_ref, tri_ref, trit_ref,
                alog_l_ref, dtb_l_ref, alog_c_ref, dtb_c_ref, yf_ref, yb_ref, st_ref):
    @pl.when(pl.program_id(1) == 0)
    def _():
        st_ref[...] = jnp.zeros_like(st_ref)

    consts = (tri_ref, trit_ref, alog_l_ref, dtb_l_ref, alog_c_ref, dtb_c_ref)
    _ssd_dir(0, xf_ref, dcf_ref, drf_ref, *consts, yf_ref, st_ref)
    _ssd_dir(1, xb_ref, dcb_ref, drb_ref, *consts, yb_ref, st_ref)


def _ssd_scan(xbc, dt3, dt_t, a_log, dt_bias, l_lat):
    nb, t_all, _ = xbc.shape
    q = SSD_CHUNK
    lc = t_all - l_lat
    fwd, bwd = _seg_orders(l_lat // q, lc // q)
    low = jnp.tril(jnp.ones((q, q), f32))
    tri = jnp.stack([low, low.T])
    trit = jnp.stack([low.T, low])
    nh2 = 2 * SSD_HEADS
    alog_l = jnp.zeros((1, LANES), f32).at[0, :nh2].set(a_log.reshape(nh2))
    dtb_l = jnp.zeros((1, LANES), f32).at[0, :nh2].set(dt_bias.reshape(nh2))

    def specs(order):
        return [pl.BlockSpec((1, q, SSD_XBC), lambda b, i: (b, order(i), 0)),
                pl.BlockSpec((1, q, LANES), lambda b, i: (b, order(i), 0)),
                pl.BlockSpec((1, nh2, q), lambda b, i: (b, 0, order(i)))]

    def const(shape):
        return pl.BlockSpec(shape, lambda b, i: (0,) * len(shape))

    return pl.pallas_call(
        _ssd_kernel,
        out_shape=[jax.ShapeDtypeStruct((nb, t_all, SSD_W), f32)] * 2,
        grid=(nb, t_all // q),
        in_specs=specs(fwd) + specs(bwd) + [const((2, q, q)), const((2, q, q)), const((1, LANES)),
                                            const((1, LANES)), const((nh2, 1)), const((nh2, 1))],
        out_specs=[pl.BlockSpec((1, q, SSD_W), lambda b, i: (b, fwd(i), 0)),
                   pl.BlockSpec((1, q, SSD_W), lambda b, i: (b, bwd(i), 0))],
        scratch_shapes=[pltpu.VMEM((2, SSD_HEADS, SSD_STATE, SSD_HEAD_DIM), f32)],
        compiler_params=_params(("parallel", "arbitrary"), 32),
        name="ssd_scan",
    )(xbc, dt3, dt_t, xbc, dt3, dt_t, tri, trit, alog_l, dtb_l,
      a_log.reshape(nh2, 1), dt_bias.reshape(nh2, 1))


def _ssd_finish_kernel(yf_ref, yb_ref, xs_ref, z_ref, dsk_ref, g_ref, o_ref):
    y = yf_ref[...] + yb_ref[...] + dsk_ref[...] * xs_ref[...]
    y = y * _silu(z_ref[...].astype(f32))
    y = y * lax.rsqrt(jnp.mean(y * y, axis=-1, keepdims=True) + NORM_EPS) * g_ref[...]
    o_ref[...] = y.astype(o_ref.dtype)


def _ssd_finish(yf2, yb2, xbc2, proj2, d_skip, norm_g):
    r, w = yf2.shape
    tm = _pick(r, (1024, 768, 512, 384, 256, 128))
    row = lambda i: (i, 0)
    return pl.pallas_call(
        _ssd_finish_kernel,
        out_shape=jax.ShapeDtypeStruct((r, w), bf16),
        grid=(r // tm,),
        in_specs=[pl.BlockSpec((tm, w), row), pl.BlockSpec((tm, w), row), pl.BlockSpec((tm, w), row),
                  pl.BlockSpec((tm, w), lambda i: (i, C_SSD_Z // SSD_W)),
                  pl.BlockSpec((1, w), lambda i: (0, 0)), pl.BlockSpec((1, w), lambda i: (0, 0))],
        out_specs=pl.BlockSpec((tm, w), row),
        compiler_params=_params(("parallel",), 40),
        name="ssd_finish",
    )(yf2, yb2, xbc2, proj2, jnp.repeat(d_skip, SSD_HEAD_DIM).reshape(1, w), norm_g.reshape(1, w))


def _router_kernel(f_ref, w_ref, b_ref, idx_ref, wt_ref):
    hi = lax.Precision.HIGHEST
    logits = jnp.dot(f_ref[...], w_ref[...], precision=hi, preferred_element_type=f32) + b_ref[...]
    lane = lax.broadcasted_iota(jnp.int32, logits.shape, 1)
    neg = jnp.float32(-jnp.inf)
    logits = jnp.where(lane < N_EXPERTS, logits, neg)
    v1 = jnp.max(logits, axis=-1, keepdims=True)
    i1 = jnp.min(jnp.where(logits == v1, lane, GATE_LANES), axis=-1, keepdims=True)
    rest = jnp.where(lane == i1, neg, logits)
    v2 = jnp.max(rest, axis=-1, keepdims=True)
    i2 = jnp.min(jnp.where(rest == v2, lane, GATE_LANES), axis=-1, keepdims=True)
    e2 = jnp.exp(v2 - v1)
    w1 = 1.0 / (1.0 + e2)
    idx_ref[...] = jnp.where(lane == 0, i1, jnp.where(lane == 1, i2, 0))
    wt_ref[...] = jnp.where(lane == 0, w1, jnp.where(lane == 1, e2 * w1, 0.0))


def _router(fin, w_router, b_router):
    r, d = fin.shape
    tm = _pick(r, (512, 384, 256, 128))
    wp = jnp.zeros((d, GATE_LANES), f32).at[:, :N_EXPERTS].set(w_router)
    bp = jnp.zeros((1, GATE_LANES), f32).at[0, :N_EXPERTS].set(b_router)
    return pl.pallas_call(
        _router_kernel,
        out_shape=[jax.ShapeDtypeStruct((r, GATE_LANES), jnp.int32), jax.ShapeDtypeStruct((r, GATE_LANES), f32)],
        grid=(r // tm,),
        in_specs=[pl.BlockSpec((tm, d), lambda i: (i, 0)),
                  pl.BlockSpec((d, GATE_LANES), lambda i: (0, 0)),
                  pl.BlockSpec((1, GATE_LANES), lambda i: (0, 0))],
        out_specs=[pl.BlockSpec((tm, GATE_LANES), lambda i: (i, 0))] * 2,
        compiler_params=_params(("parallel",), 40),
        name="router",
    )(fin, wp, bp)


MOE_TG = 512


def _route_meta(e_idx, rg):
    n_assign = e_idx.size
    e_flat = e_idx.reshape(n_assign)
    onehot = (e_flat[:, None] == jnp.arange(N_EXPERTS, dtype=jnp.int32)[None, :]).astype(jnp.int32)
    csum = jnp.cumsum(onehot, axis=0)
    counts = csum[-1]
    rank = jnp.take_along_axis(csum, e_flat[:, None], axis=1)[:, 0] - 1
    gsize = ((counts + MOE_TG - 1) // MOE_TG) * MOE_TG
    ends = jnp.cumsum(gsize)
    pos = (ends - gsize)[e_flat] + rank
    tile_start = jnp.arange(rg // MOE_TG, dtype=jnp.int32) * MOE_TG
    tile_expert = jnp.minimum(jnp.sum((tile_start[:, None] >= ends[None, :]).astype(jnp.int32), axis=1),
                              N_EXPERTS - 1).astype(jnp.int32)
    n_used = (ends[-1:] // MOE_TG).astype(jnp.int32)
    src = jnp.zeros((rg,), jnp.int32).at[pos].set(jnp.arange(n_assign, dtype=jnp.int32) // TOP_K)
    return pos.reshape(e_idx.shape).astype(jnp.int32), src, tile_expert, n_used


def _row_gather_kernel(src_ref, x_hbm, o_ref, buf_ref, sem):
    rows = buf_ref.shape[0]

    def issue(r, carry):
        pltpu.make_async_copy(x_hbm.at[pl.ds(src_ref[r], 1), :], buf_ref.at[pl.ds(r, 1), :], sem).start()
        return carry

    lax.fori_loop(0, rows, issue, 0, unroll=8)
    pltpu.make_async_copy(x_hbm.at[pl.ds(0, rows), :], buf_ref, sem).wait()
    o_ref[...] = buf_ref[...].astype(o_ref.dtype)


def _row_gather(x2, src, out_dtype):
    rg = src.shape[0]
    d = x2.shape[1]
    return pl.pallas_call(
        _row_gather_kernel,
        out_shape=jax.ShapeDtypeStruct((rg, d), out_dtype),
        grid=(rg // MOE_TG,),
        in_specs=[pl.BlockSpec((MOE_TG,), lambda i: (i,), memory_space=pltpu.SMEM),
                  pl.BlockSpec(memory_space=pl.ANY)],
        out_specs=pl.BlockSpec((MOE_TG, d), lambda i: (i, 0)),
        scratch_shapes=[pltpu.VMEM((MOE_TG, d), x2.dtype), pltpu.SemaphoreType.DMA(())],
        compiler_params=_params(("arbitrary",), 32),
        name="moe_gather",
    )(src, x2)


def _cast_on_expert_change(te_ref, used, pairs):
    i = pl.program_id(1)
    changed = jnp.logical_or(i == 0, te_ref[i] != te_ref[jnp.maximum(i - 1, 0)])

    @pl.when(jnp.logical_and(used, changed))
    def _():
        for src, dst in pairs:
            dst[...] = src[0].astype(bf16)


def _gmm_swiglu_kernel(te_ref, nu_ref, a_ref, w1_ref, w3_ref, o_ref, w1b_ref, w3b_ref):
    used = pl.program_id(1) < nu_ref[0]
    _cast_on_expert_change(te_ref, used, ((w1_ref, w1b_ref), (w3_ref, w3b_ref)))

    @pl.when(used)
    def _():
        a = a_ref[...]
        g = jnp.dot(a, w1b_ref[...], preferred_element_type=f32)
        u = jnp.dot(a, w3b_ref[...], preferred_element_type=f32)
        o_ref[...] = (_silu(g) * u).astype(o_ref.dtype)

    @pl.when(jnp.logical_not(used))
    def _():
        o_ref[...] = jnp.zeros_like(o_ref)


def _gmm_out_kernel(te_ref, nu_ref, a_ref, w_ref, o_ref, wb_ref):
    used = pl.program_id(1) < nu_ref[0]
    _cast_on_expert_change(te_ref, used, ((w_ref, wb_ref),))

    @pl.when(used)
    def _():
        o_ref[...] = jnp.dot(a_ref[...], wb_ref[...], preferred_element_type=f32)

    @pl.when(jnp.logical_not(used))
    def _():
        o_ref[...] = jnp.zeros_like(o_ref)


def _used(i, nu):
    return jnp.minimum(i, nu[0] - 1)


def _gmm_swiglu(xs, w1, w3, layer, tile_expert, n_used, tn=1024):
    rg, k = xs.shape
    n = w1.shape[-1]
    wspec = pl.BlockSpec((None, 1, k, tn), lambda j, i, te, nu: (layer, te[_used(i, nu)], 0, j),
                         pipeline_mode=pl.Buffered(1))
    return pl.pallas_call(
        _gmm_swiglu_kernel,
        out_shape=jax.ShapeDtypeStruct((rg, n), bf16),
        grid_spec=pltpu.PrefetchScalarGridSpec(
            num_scalar_prefetch=2, grid=(n // tn, rg // MOE_TG),
            in_specs=[pl.BlockSpec((MOE_TG, k), lambda j, i, te, nu: (_used(i, nu), 0)), wspec, wspec],
            out_specs=pl.BlockSpec((MOE_TG, tn), lambda j, i, te, nu: (i, j)),
            scratch_shapes=[pltpu.VMEM((k, tn), bf16)] * 2),
        compiler_params=_params(("parallel", "arbitrary"), 56),
        name="moe_up",
    )(tile_expert, n_used, xs, w1, w3)


def _gmm_out(hmid, w2, layer, tile_expert, n_used, tn=1024):
    rg, k = hmid.shape
    n = w2.shape[-1]
    return pl.pallas_call(
        _gmm_out_kernel,
        out_shape=jax.ShapeDtypeStruct((rg, n), f32),
        grid_spec=pltpu.PrefetchScalarGridSpec(
            num_scalar_prefetch=2, grid=(n // tn, rg // MOE_TG),
            in_specs=[pl.BlockSpec((MOE_TG, k), lambda j, i, te, nu: (_used(i, nu), 0)),
                      pl.BlockSpec((None, 1, k, tn), lambda j, i, te, nu: (layer, te[_used(i, nu)], 0, j),
                                   pipeline_mode=pl.Buffered(1))],
            out_specs=pl.BlockSpec((MOE_TG, tn), lambda j, i, te, nu: (i, j)),
            scratch_shapes=[pltpu.VMEM((k, tn), bf16)]),
        compiler_params=_params(("parallel", "arbitrary"), 56),
        name="moe_down",
    )(tile_expert, n_used, hmid, w2)


def _combine_kernel(p0_ref, p1_ref, ys_hbm, x_ref, m_ref, wt_ref, o_ref, b0_ref, b1_ref, sems, *, k_mod):
    rows = b0_ref.shape[0]

    def issue(r, carry):
        pltpu.make_async_copy(ys_hbm.at[pl.ds(p0_ref[r], 1), :], b0_ref.at[pl.ds(r, 1), :], sems.at[0]).start()
        pltpu.make_async_copy(ys_hbm.at[pl.ds(p1_ref[r], 1), :], b1_ref.at[pl.ds(r, 1), :], sems.at[1]).start()
        return carry

    lax.fori_loop(0, rows, issue, 0, unroll=8)
    pltpu.make_async_copy(ys_hbm.at[pl.ds(0, rows), :], b0_ref, sems.at[0]).wait()
    pltpu.make_async_copy(ys_hbm.at[pl.ds(0, rows), :], b1_ref, sems.at[1]).wait()
    mixed = wt_ref[:, 0:1] * b0_ref[...] + wt_ref[:, 1:2] * b1_ref[...]
    o_ref[...] = x_ref[...] + m_ref[0, k_mod:k_mod + 1, :] * mixed


def _moe_combine(ys, pos, wts, x2, modtab, nb, t_all, l_lat, k_mod):
    r, d = x2.shape
    tm = _pick(math.gcd(l_lat, t_all - l_lat), (256, 128))
    tpb = t_all // tm
    n_lat_tiles = l_lat // tm
    sspec = pl.BlockSpec((tm,), lambda i: (i,), memory_space=pltpu.SMEM)
    return pl.pallas_call(
        functools.partial(_combine_kernel, k_mod=k_mod),
        out_shape=jax.ShapeDtypeStruct((r, d), f32),
        grid=(r // tm,),
        in_specs=[sspec, sspec, pl.BlockSpec(memory_space=pl.ANY),
                  pl.BlockSpec((tm, d), lambda i: (i, 0)),
                  pl.BlockSpec((1, 6, d), lambda i: (jnp.where(i % tpb >= n_lat_tiles, nb, i // tpb), 0, 0)),
                  pl.BlockSpec((tm, GATE_LANES), lambda i: (i, 0))],
        out_specs=pl.BlockSpec((tm, d), lambda i: (i, 0)),
        scratch_shapes=[pltpu.VMEM((tm, d), f32), pltpu.VMEM((tm, d), f32), pltpu.SemaphoreType.DMA((2,))],
        input_output_aliases={3: 0},
        compiler_params=_params(("arbitrary",), 32),
        name="moe_combine",
    )(pos[:, 0], pos[:, 1], ys, x2, modtab, wts)


def _moe(fin32, xall, w_router, b_router, w1, w3, w2, layer, modtab, nb, t_all, l_lat):
    rows = fin32.shape[0]
    rg = TOP_K * rows + N_EXPERTS * MOE_TG
    idx, wts = _router(fin32, w_router, b_router)
    pos, src, tile_expert, n_used = _route_meta(idx[:, :TOP_K], rg)
    xs = _row_gather(fin32, src, bf16)
    hmid = _gmm_swiglu(xs, w1, w3, layer, tile_expert, n_used)
    ys = _gmm_out(hmid, w2, layer, tile_expert, n_used)
    return _moe_combine(ys, pos, wts, xall, modtab, nb, t_all, l_lat, 5)


def _final_norm_kernel(x_ref, g_ref, o_ref):
    x = x_ref[0]
    o_ref[0] = x * lax.rsqrt(jnp.mean(x * x, axis=-1, keepdims=True) + NORM_EPS) * g_ref[...]


def _final_norm(x3, gain, l_lat):
    nb, t_all, d = x3.shape
    tr = _pick(math.gcd(l_lat, t_all), (512, 256, 128))
    return pl.pallas_call(
        _final_norm_kernel,
        out_shape=jax.ShapeDtypeStruct((nb, l_lat, d), f32),
        grid=(nb, l_lat // tr),
        in_specs=[pl.BlockSpec((1, tr, d), lambda b, j: (b, j, 0)),
                  pl.BlockSpec((1, d), lambda b, j: (0, 0))],
        out_specs=pl.BlockSpec((1, tr, d), lambda b, j: (b, j, 0)),
        compiler_params=_params(("parallel", "parallel"), 32),
        name="final_norm",
    )(x3, gain.reshape(1, d))


def _split_w_in(w):
    dt0 = C_SSD_XBC + SSD_XBC
    dt1 = dt0 + 2 * SSD_HEADS
    w_main = jnp.concatenate([w[:, :dt0], w[:, dt1:]], axis=1)
    w_dt = jnp.pad(w[:, dt0:dt1], ((0, 0), (0, LANES - (dt1 - dt0))))
    return w_main, w_dt


def _block_diag(w):
    h, bw, _ = w.shape
    eye = jnp.eye(h, dtype=w.dtype)
    return (eye[:, None, :, None] * w[:, :, None, :]).reshape(h * bw, h * bw)


def kernel(x, c, ctx, c_ctx, norm_mix, norm_ffn, ada_w, ada_b, w_in, hy_conv_w, hy_conv_b, hy_w1, hy_b1, hy_w2, hy_b2, hy_freq, hy_w3, hy_b3, hy_decay, hy_bias, lru_conv_w, lru_conv_b, lru_wa, lru_ba, lru_wx, lru_bx, lru_lambda, ssd_conv_w, ssd_conv_b, ssd_a_log, ssd_dt_bias, ssd_d, ssd_norm, att_q_norm, att_k_norm, w_branch, w_gate, b_gate, w_out, ffn_w1, ffn_w3, ffn_w2, moe_router, moe_router_b, moe_w1, moe_w3, moe_w2, norm_final):
    nb, l_lat, d = x.shape
    lc = ctx.shape[1]
    t_all = l_lat + lc
    rows = nb * t_all
    depth = norm_mix.shape[0]
    tm = _pick(t_all, (768, 384, 256, 128))

    mod_rows = -(-(nb + 1) // SUBLANES) * SUBLANES
    cvec = jnp.zeros((mod_rows, d), f32).at[:nb].set(c).at[nb].set(c_ctx)
    mods = _mods(cvec, ada_w, ada_b).reshape(depth, mod_rows, 6, d)

    cos2, sin2 = _rope_tables(l_lat, lc)
    dft_l, cs_l, nf_l = _dft_mats(l_lat)
    dft_c, cs_c, nf_c = _dft_mats(lc)

    xall = jnp.concatenate([x, ctx], axis=1).reshape(rows, d)

    for l in range(depth):
        with_ctx = l < depth - 1
        modtab = mods[l]
        h = _normmod(xall, norm_mix[l], modtab, nb, t_all, l_lat, 0, 1)
        w_main, w_dt = _split_w_in(w_in[l])
        proj2 = _mm(h, w_main, bf16, t_all, 512)
        proj = proj2.reshape(nb, t_all, PROJ_W)
        dt3 = _mm(h, w_dt, f32, t_all, LANES).reshape(nb, t_all, LANES)

        hy_args = (hy_w1[l], hy_b1[l], hy_w2[l], hy_b2[l], hy_freq[l], hy_w3[l], hy_b3[l], hy_decay[l], hy_bias[l])
        hu = _dwconv(proj, C_HY, HY_IN, hy_conv_w[l], hy_conv_b[l], (HY_SHORT - 1) // 2, l_lat)
        k_l = _hyena_kspec(l_lat, cs_l, nf_l, *hy_args)
        z_l = _hyconv(hu, (0, 0), hu, (0, 1), l_lat, dft_l, k_l[0])
        hy_l = _hyconv(z_l, (0, 0), hu, (0, 2), l_lat, dft_l, k_l[1])
        if with_ctx:
            k_c = _hyena_kspec(lc, cs_c, nf_c, *hy_args)
            z_c = _hyconv(hu, (l_lat // lc, 0), hu, (l_lat // lc, 1), lc, dft_c, k_c[0])
            hy_c = _hyconv(z_c, (0, 0), hu, (l_lat // lc, 2), lc, dft_c, k_c[1])
        else:
            hy_c = jnp.zeros((nb, lc, HY_W), bf16)
        y_hy = jnp.concatenate([hy_l, hy_c], axis=1).reshape(rows, HY_W)

        xc = _dwconv(proj, C_LRU_X, LRU_W, lru_conv_w[l], lru_conv_b[l], LRU_CONV // 2, l_lat)
        w_all = jnp.concatenate([_block_diag(lru_wa[l, 0]), _block_diag(lru_wx[l, 0]),
                                 _block_diag(lru_wa[l, 1]), _block_diag(lru_wx[l, 1])], axis=1).astype(bf16)
        bias_all = jnp.concatenate([lru_ba[l, 0].reshape(-1), lru_bx[l, 0].reshape(-1),
                                    lru_ba[l, 1].reshape(-1), lru_bx[l, 1].reshape(-1)]).reshape(1, 4 * LRU_W)
        a2, b2 = _lru_coefs(xc.reshape(rows, LRU_W), w_all, bias_all, lru_lambda[l])
        hf, hb = _lru_scan(a2.reshape(2, nb, t_all, LRU_W), b2.reshape(2, nb, t_all, LRU_W), l_lat)
        y_lru = _lru_finish(proj2, hf.reshape(rows, LRU_W), hb.reshape(rows, LRU_W))

        xbc = _dwconv(proj, C_SSD_XBC, SSD_XBC, ssd_conv_w[l], ssd_conv_b[l], SSD_CONV // 2, l_lat, act=True)
        dt_t = dt3[:, :, :2 * SSD_HEADS].transpose(0, 2, 1)
        yf, yb = _ssd_scan(xbc, dt3, dt_t, ssd_a_log[l], ssd_dt_bias[l], l_lat)
        y_ssd = _ssd_finish(yf.reshape(rows, SSD_W), yb.reshape(rows, SSD_W), xbc.reshape(rows, SSD_XBC),
                            proj2, ssd_d[l], ssd_norm[l])

        y_att = _attention(proj, cos2, sin2, att_q_norm[l], att_k_norm[l], l_lat).reshape(rows, ATT_Q)

        merged = _merge(h, [y_hy, y_lru, y_ssd, y_att], w_gate, b_gate[l], w_branch, l, tm)
        xall = _mm_resid(merged, w_out, xall, modtab, nb, t_all, l_lat, 2, tm, 1024, lead=(l,))

        j = l // 2
        if l % 2 == 0:
            fin = _normmod(xall, norm_ffn[l], modtab, nb, t_all, l_lat, 3, 4)
            padc = FF_DENSE_PAD - FF_DENSE
            w1 = jnp.pad(ffn_w1[j], ((0, 0), (0, padc)))
            w3 = jnp.pad(ffn_w3[j], ((0, 0), (0, padc)))
            w2 = jnp.pad(ffn_w2[j], ((0, padc), (0, 0)))
            hmid = _mm_swiglu(fin, w1, w3, tm, 512)
            xall = _mm_resid(hmid, w2, xall, modtab, nb, t_all, l_lat, 5, _pick(t_all, (384, 128)), 512, vmem=56)
        else:
            fin32 = _normmod(xall, norm_ffn[l], modtab, nb, t_all, l_lat, 3, 4, out_dtype=f32)
            xall = _moe(fin32, xall, moe_router[j], moe_router_b[j], moe_w1, moe_w3, moe_w2, j,
                        modtab, nb, t_all, l_lat)

    return _final_norm(xall.reshape(nb, t_all, d), norm_final, l_lat)
```

```python
import functools
import math

import jax
import jax.numpy as jnp
from jax import lax
from jax.experimental import pallas as pl
from jax.experimental.pallas import tpu as pltpu

f32 = jnp.float32
bf16 = jnp.bfloat16

D_MODEL = 2048
DEPTH = 4
GRID_W = 64
NORM_EPS = 1e-6
HY_W = 512
HY_ORDER = 2
HY_SHORT = 3
HY_BANDS = 16
HY_IN = (HY_ORDER + 1) * HY_W
LRU_W = 512
LRU_HEADS = 8
LRU_BW = LRU_W // LRU_HEADS
LRU_CONV = 4
LRU_C = 8.0
SSD_HEADS = 8
SSD_HEAD_DIM = 64
SSD_W = SSD_HEADS * SSD_HEAD_DIM
SSD_GROUPS = 2
SSD_STATE = 128
SSD_CONV = 4
SSD_CHUNK = 128
SSD_XBC = SSD_W + 2 * SSD_GROUPS * SSD_STATE
ATT_HEADS = 8
ATT_KV_HEADS = 2
ATT_HEAD_DIM = 128
ROPE_THETA = 10000.0
ATT_Q = ATT_HEADS * ATT_HEAD_DIM
ATT_KV = ATT_KV_HEADS * ATT_HEAD_DIM
FF_DENSE = 5504
N_EXPERTS = 8
TOP_K = 2

LANES = 128
SUBLANES = 8
MIB = 2 ** 20

C_HY = 0
C_LRU_G = C_HY + HY_IN
C_LRU_X = C_LRU_G + LRU_W
C_SSD_Z = C_LRU_X + LRU_W
C_SSD_XBC = C_SSD_Z + SSD_W
C_Q = C_SSD_XBC + SSD_XBC
C_K = C_Q + ATT_Q
C_V = C_K + ATT_KV
PROJ_W = C_V + ATT_KV
FF_DENSE_PAD = 5632
GATE_LANES = LANES


def _params(sem, vmem_mib):
    return pltpu.CompilerParams(dimension_semantics=sem, vmem_limit_bytes=vmem_mib * MIB)


def _pick(n, cands):
    for c in cands:
        if n % c == 0:
            return c
    raise ValueError(f"no tile for {n} in {cands}")


def _silu(v):
    return v * jax.nn.sigmoid(v)


def _softplus(v):
    return jnp.maximum(v, 0.0) + jnp.log1p(jnp.exp(-jnp.abs(v)))


def _mods_kernel(c_ref, w_ref, b_ref, o_ref):
    s = _silu(c_ref[...]).astype(bf16)
    o_ref[0] = jnp.dot(s, w_ref[0].astype(bf16), preferred_element_type=f32) + b_ref[0]


def _mods(cvec, ada_w, ada_b):
    depth, d, n = ada_w.shape
    rows = cvec.shape[0]
    tn = 1024
    return pl.pallas_call(
        _mods_kernel,
        out_shape=jax.ShapeDtypeStruct((depth, rows, n), f32),
        grid=(depth, n // tn),
        in_specs=[pl.BlockSpec((rows, d), lambda l, j: (0, 0)),
                  pl.BlockSpec((1, d, tn), lambda l, j: (l, 0, j)),
                  pl.BlockSpec((1, 1, tn), lambda l, j: (l, 0, j))],
        out_specs=pl.BlockSpec((1, rows, tn), lambda l, j: (l, 0, j)),
        compiler_params=_params(("parallel", "parallel"), 40),
        name="mods",
    )(cvec, ada_w, ada_b.reshape(depth, 1, n))


def _row_mod(ml_ref, mc_ref, k, is_ctx):
    return jnp.where(is_ctx, mc_ref[0, k:k + 1, :], ml_ref[0, k:k + 1, :])


def _is_ctx_rows(tm, tpb, l_lat, axis):
    t = (pl.program_id(axis) % tpb) * tm + lax.broadcasted_iota(jnp.int32, (tm, 1), 0)
    return t >= l_lat


def _normmod_kernel(x_ref, g_ref, ml_ref, mc_ref, o_ref, *, tm, tpb, l_lat, k_shift, k_scale):
    x = x_ref[...]
    y = x * lax.rsqrt(jnp.mean(x * x, axis=-1, keepdims=True) + NORM_EPS) * g_ref[...]
    is_ctx = _is_ctx_rows(tm, tpb, l_lat, 0)
    shift = _row_mod(ml_ref, mc_ref, k_shift, is_ctx)
    scale = _row_mod(ml_ref, mc_ref, k_scale, is_ctx)
    o_ref[...] = (y * (1.0 + scale) + shift).astype(o_ref.dtype)


def _normmod(x2, gain, modtab, nb, t_all, l_lat, k_shift, k_scale, out_dtype=bf16):
    r, d = x2.shape
    tm = _pick(t_all, (768, 512, 384, 256, 128))
    tpb = t_all // tm
    return pl.pallas_call(
        functools.partial(_normmod_kernel, tm=tm, tpb=tpb, l_lat=l_lat, k_shift=k_shift, k_scale=k_scale),
        out_shape=jax.ShapeDtypeStruct((r, d), out_dtype),
        grid=(r // tm,),
        in_specs=[pl.BlockSpec((tm, d), lambda i: (i, 0)),
                  pl.BlockSpec((1, d), lambda i: (0, 0)),
                  pl.BlockSpec((1, 6, d), lambda i: (i // tpb, 0, 0)),
                  pl.BlockSpec((1, 6, d), lambda i: (nb, 0, 0))],
        out_specs=pl.BlockSpec((tm, d), lambda i: (i, 0)),
        compiler_params=_params(("parallel",), 56),
        name="normmod",
    )(x2, gain.reshape(1, d), modtab, modtab)


def _cast_on_first_row_tile(pairs):
    @pl.when(pl.program_id(1) == 0)
    def _():
        for src, dst in pairs:
            dst[...] = src[...].astype(bf16)


def _mm_kernel(a_ref, w_ref, o_ref, wb_ref):
    _cast_on_first_row_tile(((w_ref, wb_ref),))
    o_ref[...] = jnp.dot(a_ref[...], wb_ref[...], preferred_element_type=f32).astype(o_ref.dtype)


def _wspec(w, lead, tn):
    k = w.shape[-2]
    nl = len(lead)
    return pl.BlockSpec((None,) * nl + (k, tn), lambda j, i: tuple(lead) + (0, j), pipeline_mode=pl.Buffered(1))


def _mm3_kernel(a_ref, w_ref, o_ref):
    a = a_ref[...]
    w = w_ref[...]
    ah = a.astype(bf16)
    al = (a - ah.astype(f32)).astype(bf16)
    wh = w.astype(bf16)
    wl = (w - wh.astype(f32)).astype(bf16)
    o_ref[...] = (jnp.dot(ah, wh, preferred_element_type=f32) + jnp.dot(ah, wl, preferred_element_type=f32)
                  + jnp.dot(al, wh, preferred_element_type=f32))


def _mm3(a, w):
    m, k = a.shape
    n = w.shape[1]
    tm = _pick(m, (256, 128))
    return pl.pallas_call(
        _mm3_kernel,
        out_shape=jax.ShapeDtypeStruct((m, n), f32),
        grid=(m // tm,),
        in_specs=[pl.BlockSpec((tm, k), lambda i: (i, 0)),
                  pl.BlockSpec((k, n), lambda i: (0, 0))],
        out_specs=pl.BlockSpec((tm, n), lambda i: (i, 0)),
        compiler_params=_params(("parallel",), 48),
        name="mm3",
    )(a, w)


def _mm(a, w, out_dtype, tm, tn, vmem=48):
    m, k = a.shape
    n = w.shape[1]
    return pl.pallas_call(
        _mm_kernel,
        out_shape=jax.ShapeDtypeStruct((m, n), out_dtype),
        grid=(n // tn, m // tm),
        in_specs=[pl.BlockSpec((tm, k), lambda j, i: (i, 0)),
                  _wspec(w, (), tn)],
        out_specs=pl.BlockSpec((tm, tn), lambda j, i: (i, j)),
        scratch_shapes=[pltpu.VMEM((k, tn), bf16)],
        compiler_params=_params(("parallel", "arbitrary"), vmem),
        name="mm",
    )(a, w)


def _mm_swiglu_kernel(a_ref, w1_ref, w3_ref, o_ref, w1b_ref, w3b_ref):
    _cast_on_first_row_tile(((w1_ref, w1b_ref), (w3_ref, w3b_ref)))
    a = a_ref[...]
    g = jnp.dot(a, w1b_ref[...], preferred_element_type=f32)
    u = jnp.dot(a, w3b_ref[...], preferred_element_type=f32)
    o_ref[...] = (_silu(g) * u).astype(o_ref.dtype)


def _mm_swiglu(a, w1, w3, tm, tn, vmem=48):
    m, k = a.shape
    n = w1.shape[1]
    wspec = _wspec(w1, (), tn)
    return pl.pallas_call(
        _mm_swiglu_kernel,
        out_shape=jax.ShapeDtypeStruct((m, n), bf16),
        grid=(n // tn, m // tm),
        in_specs=[pl.BlockSpec((tm, k), lambda j, i: (i, 0)), wspec, wspec],
        out_specs=pl.BlockSpec((tm, tn), lambda j, i: (i, j)),
        scratch_shapes=[pltpu.VMEM((k, tn), bf16)] * 2,
        compiler_params=_params(("parallel", "arbitrary"), vmem),
        name="mm_swiglu",
    )(a, w1, w3)


def _mm_resid_kernel(a_ref, w_ref, x_ref, ml_ref, mc_ref, o_ref, wb_ref, *, tm, tpb, l_lat, k_mod):
    _cast_on_first_row_tile(((w_ref, wb_ref),))
    acc = jnp.dot(a_ref[...], wb_ref[...], preferred_element_type=f32)
    mod = _row_mod(ml_ref, mc_ref, k_mod, _is_ctx_rows(tm, tpb, l_lat, 1))
    o_ref[...] = x_ref[...] + mod * acc


def _mm_resid(a, w, x2, modtab, nb, t_all, l_lat, k_mod, tm, tn, lead=(), vmem=48):
    m, k = a.shape
    n = w.shape[-1]
    tpb = t_all // tm
    return pl.pallas_call(
        functools.partial(_mm_resid_kernel, tm=tm, tpb=tpb, l_lat=l_lat, k_mod=k_mod),
        out_shape=jax.ShapeDtypeStruct((m, n), f32),
        grid=(n // tn, m // tm),
        in_specs=[pl.BlockSpec((tm, k), lambda j, i: (i, 0)),
                  _wspec(w, lead, tn),
                  pl.BlockSpec((tm, tn), lambda j, i: (i, j)),
                  pl.BlockSpec((1, 6, tn), lambda j, i: (i // tpb, 0, j)),
                  pl.BlockSpec((1, 6, tn), lambda j, i: (nb, 0, j))],
        out_specs=pl.BlockSpec((tm, tn), lambda j, i: (i, j)),
        scratch_shapes=[pltpu.VMEM((k, tn), bf16)],
        input_output_aliases={2: 0},
        compiler_params=_params(("parallel", "arbitrary"), vmem),
        name="mm_resid",
    )(a, w, x2, modtab, modtab)


def _merge_kernel(h_ref, y0_ref, y1_ref, y2_ref, y3_ref, wg_ref, bg_ref, wb_ref, o_ref, wgb_ref, wbb_ref,
                  *, offs):
    _cast_on_first_row_tile(((wg_ref, wgb_ref), (wb_ref, wbb_ref)))
    h = h_ref[...]
    acc = None
    for i, y_ref in enumerate((y0_ref, y1_ref, y2_ref, y3_ref)):
        g = jnp.dot(h, wgb_ref[i], preferred_element_type=f32) + bg_ref[i]
        p = jnp.dot(y_ref[...], wbb_ref[offs[i]:offs[i + 1], :], preferred_element_type=f32)
        term = jax.nn.sigmoid(g) * p
        acc = term if acc is None else acc + term
    o_ref[...] = acc.astype(o_ref.dtype)


def _merge(h, ys, wg_all, bg, wb_all, layer, tm, tn=256):
    m, d = h.shape
    n = wg_all.shape[-1]
    kb = wb_all.shape[-2]
    offs = [0]
    for y in ys:
        offs.append(offs[-1] + y.shape[1])
    one = pl.Buffered(1)
    in_specs = [pl.BlockSpec((tm, d), lambda j, i: (i, 0))]
    in_specs += [pl.BlockSpec((tm, y.shape[1]), lambda j, i: (i, 0)) for y in ys]
    in_specs += [pl.BlockSpec((None, 4, d, tn), lambda j, i: (layer, 0, 0, j), pipeline_mode=one),
                 pl.BlockSpec((4, 1, tn), lambda j, i: (0, 0, j)),
                 pl.BlockSpec((None, kb, tn), lambda j, i: (layer, 0, j), pipeline_mode=one)]
    return pl.pallas_call(
        functools.partial(_merge_kernel, offs=tuple(offs)),
        out_shape=jax.ShapeDtypeStruct((m, n), bf16),
        grid=(n // tn, m // tm),
        in_specs=in_specs,
        out_specs=pl.BlockSpec((tm, tn), lambda j, i: (i, j)),
        scratch_shapes=[pltpu.VMEM((4, d, tn), bf16), pltpu.VMEM((kb, tn), bf16)],
        compiler_params=_params(("parallel", "arbitrary"), 56),
        name="merge",
    )(h, *ys, wg_all, bg.reshape(4, 1, n), wb_all)


def _norm_rope(x, gain, cos, sin):
    y = x * lax.rsqrt(jnp.mean(x * x, axis=-1, keepdims=True) + NORM_EPS) * gain
    lane = lax.broadcasted_iota(jnp.int32, y.shape, 1)
    partner = jnp.where(lane % 2 == 0, pltpu.roll(y, ATT_HEAD_DIM - 1, 1), pltpu.roll(y, 1, 1))
    return y * cos + partner * sin


def _attn_kernel(q_ref, k_ref, v_ref, cos_ref, sin_ref, qg_ref, kg_ref, o_ref, ks_ref, vs_ref,
                 *, tq, l_lat, n_lat_blocks):
    qi = pl.program_id(2)
    grp = ATT_HEADS // ATT_KV_HEADS
    hd = ATT_HEAD_DIM

    @pl.when(qi == 0)
    def _():
        ks_ref[...] = _norm_rope(k_ref[0].astype(f32), kg_ref[...], cos_ref[...], sin_ref[...]).astype(bf16)
        vs_ref[:, :hd] = v_ref[0].astype(bf16)
        vs_ref[:, hd:] = jnp.ones((vs_ref.shape[0], hd), bf16)

    r0 = pl.multiple_of(qi * tq, tq)
    cos = cos_ref[pl.ds(r0, tq), :]
    sin = sin_ref[pl.ds(r0, tq), :]
    q = q_ref[0].astype(f32)
    sc = (hd ** -0.5) * math.log2(math.e)
    qhs = [(_norm_rope(q[:, h * hd:(h + 1) * hd], qg_ref[...], cos, sin) * sc).astype(bf16) for h in range(grp)]

    def attend(ks, vs):
        for h in range(grp):
            s = lax.dot_general(qhs[h], ks, (((1,), (1,)), ((), ())), preferred_element_type=f32)
            p = jnp.exp2(s - jnp.max(s, axis=-1, keepdims=True)).astype(bf16)
            oe = jnp.dot(p, vs, preferred_element_type=f32)
            o_ref[0, :, h * hd:(h + 1) * hd] = (oe[:, :hd] / oe[:, hd:hd + 1]).astype(o_ref.dtype)

    @pl.when(qi < n_lat_blocks)
    def _():
        attend(ks_ref[...], vs_ref[...])

    @pl.when(qi >= n_lat_blocks)
    def _():
        attend(ks_ref[l_lat:, :], vs_ref[l_lat:, :])


def _attention(proj, cos2, sin2, q_gain, k_gain, l_lat):
    nb, t_all, _ = proj.shape
    lc = t_all - l_lat
    tq = lc if (l_lat % lc == 0 and lc <= 256) else _pick(math.gcd(l_lat, lc), (256, 128))
    grp = ATT_HEADS // ATT_KV_HEADS
    qw = grp * ATT_HEAD_DIM
    hd = ATT_HEAD_DIM
    return pl.pallas_call(
        functools.partial(_attn_kernel, tq=tq, l_lat=l_lat, n_lat_blocks=l_lat // tq),
        out_shape=jax.ShapeDtypeStruct((nb, t_all, ATT_Q), bf16),
        grid=(nb, ATT_KV_HEADS, t_all // tq),
        in_specs=[pl.BlockSpec((1, tq, qw), lambda b, g, i: (b, i, C_Q // qw + g)),
                  pl.BlockSpec((1, t_all, hd), lambda b, g, i: (b, 0, C_K // hd + g)),
                  pl.BlockSpec((1, t_all, hd), lambda b, g, i: (b, 0, C_V // hd + g)),
                  pl.BlockSpec((t_all, hd), lambda b, g, i: (0, 0)),
                  pl.BlockSpec((t_all, hd), lambda b, g, i: (0, 0)),
                  pl.BlockSpec((1, hd), lambda b, g, i: (0, 0)),
                  pl.BlockSpec((1, hd), lambda b, g, i: (0, 0))],
        out_specs=pl.BlockSpec((1, tq, qw), lambda b, g, i: (b, i, g)),
        scratch_shapes=[pltpu.VMEM((t_all, hd), bf16), pltpu.VMEM((t_all, 2 * hd), bf16)],
        compiler_params=_params(("parallel", "parallel", "arbitrary"), 56),
        name="attention",
    )(proj, proj, proj, cos2, sin2, q_gain.reshape(1, hd), k_gain.reshape(1, hd))


def _rope_tables(l_lat, lc):
    axis_dim = ATT_HEAD_DIM // 2
    rows = l_lat // GRID_W
    row_id = jnp.repeat(jnp.arange(rows, dtype=f32), GRID_W)
    col_id = jnp.tile(jnp.arange(GRID_W, dtype=f32), rows)
    inv = ROPE_THETA ** (-jnp.arange(0, axis_dim, 2, dtype=f32) / axis_dim)
    ang = jnp.concatenate([row_id[:, None] * inv, col_id[:, None] * inv], axis=-1)
    cos = jnp.repeat(jnp.cos(ang), 2, axis=-1)
    sin = jnp.repeat(jnp.sin(ang), 2, axis=-1)
    sign = jnp.tile(jnp.array([-1.0, 1.0], f32), ATT_HEAD_DIM // 2)
    cos2 = jnp.concatenate([cos, jnp.ones((lc, ATT_HEAD_DIM), f32)], axis=0)
    sin2 = jnp.concatenate([sin * sign, jnp.zeros((lc, ATT_HEAD_DIM), f32)], axis=0)
    return cos2, sin2


def _dwconv_kernel(u_ref, w_ref, b_ref, o_ref, *, ksz, pad_left, l_lat, t_all, act):
    u = u_ref[0].astype(f32)
    t = lax.broadcasted_iota(jnp.int32, (t_all, 1), 0)
    is_ctx = t >= l_lat
    t_loc = jnp.where(is_ctx, t - l_lat, t)
    seg_len = jnp.where(is_ctx, t_all - l_lat, l_lat)
    acc = None
    for k in range(ksz):
        s = k - pad_left
        if s == 0:
            term = u
        else:
            shifted = pltpu.roll(u, (t_all - s) % t_all, 0)
            term = jnp.where((t_loc + s >= 0) & (t_loc + s < seg_len), shifted, 0.0)
        term = term * w_ref[k:k + 1, :]
        acc = term if acc is None else acc + term
    acc = acc + b_ref[...]
    if act:
        acc = _silu(acc)
    o_ref[0] = acc.astype(o_ref.dtype)


def _dwconv(proj, col0, width, w, b, pad_left, l_lat, act=False):
    nb, t_all, _ = proj.shape
    ksz = w.shape[0]
    cw = 256
    return pl.pallas_call(
        functools.partial(_dwconv_kernel, ksz=ksz, pad_left=pad_left, l_lat=l_lat, t_all=t_all, act=act),
        out_shape=jax.ShapeDtypeStruct((nb, t_all, width), f32),
        grid=(nb, width // cw),
        in_specs=[pl.BlockSpec((1, t_all, cw), lambda bi, c: (bi, 0, col0 // cw + c)),
                  pl.BlockSpec((ksz, cw), lambda bi, c: (0, c)),
                  pl.BlockSpec((1, cw), lambda bi, c: (0, c))],
        out_specs=pl.BlockSpec((1, t_all, cw), lambda bi, c: (bi, 0, c)),
        compiler_params=_params(("parallel", "parallel"), 48),
        name="dwconv",
    )(proj, w, b.reshape(1, width))


def _hyconv_kernel(u_ref, g_ref, m_ref, mi_ref, k_ref, o_ref, ub_ref, acc_ref, *, fc, nf):
    f = pl.program_id(2)

    @pl.when(f == 0)
    def _():
        ub_ref[...] = u_ref[0].astype(bf16)
        acc_ref[...] = jnp.zeros_like(acc_ref)

    spec = jnp.dot(m_ref[0], ub_ref[...], preferred_element_type=f32)
    a, bv = spec[:fc], spec[fc:]
    kk = k_ref[0]
    kr, ki = kk[:fc], kk[fc:]
    y = jnp.concatenate([a * kr + bv * ki, a * ki - bv * kr], axis=0).astype(bf16)
    acc_ref[...] += jnp.dot(mi_ref[0], y, preferred_element_type=f32)

    @pl.when(f == nf - 1)
    def _():
        o_ref[0] = (acc_ref[...] * g_ref[0]).astype(o_ref.dtype)


def _hyconv(u, u_blk, gate, gate_blk, lseg, dft, kspec):
    m_mat, mi_mat = dft
    nf, fc2, _ = m_mat.shape
    fc = fc2 // 2
    nb = u.shape[0]
    cw = HY_W
    return pl.pallas_call(
        functools.partial(_hyconv_kernel, fc=fc, nf=nf),
        out_shape=jax.ShapeDtypeStruct((nb, lseg, cw), bf16),
        grid=(nb, 1, nf),
        in_specs=[pl.BlockSpec((1, lseg, cw), lambda b, c, f: (b,) + u_blk),
                  pl.BlockSpec((1, lseg, cw), lambda b, c, f: (b,) + gate_blk),
                  pl.BlockSpec((1, fc2, lseg), lambda b, c, f: (f, 0, 0)),
                  pl.BlockSpec((1, lseg, fc2), lambda b, c, f: (f, 0, 0)),
                  pl.BlockSpec((1, fc2, cw), lambda b, c, f: (f, 0, 0))],
        out_specs=pl.BlockSpec((1, lseg, cw), lambda b, c, f: (b, 0, 0)),
        scratch_shapes=[pltpu.VMEM((lseg, cw), bf16), pltpu.VMEM((lseg, cw), f32)],
        compiler_params=_params(("parallel", "arbitrary", "arbitrary"), 52),
        name="hyconv",
    )(u, gate, m_mat, mi_mat, kspec)


def _dft_cs(lseg):
    n = 2 * lseg
    f = jnp.arange(lseg, dtype=jnp.int32)[:, None]
    s = jnp.arange(lseg, dtype=jnp.int32)[None, :]
    ph = ((2 * f + 1) * s) % (2 * n)
    ang = ph.astype(f32) * (math.pi / n)
    return jnp.cos(ang), jnp.sin(ang)


def _dft_mats(lseg):
    fc = min(256, lseg // 2)
    nf = lseg // fc
    c, s = _dft_cs(lseg)
    c3 = c.reshape(nf, fc, lseg)
    s3 = s.reshape(nf, fc, lseg)
    m_mat = jnp.concatenate([c3, s3], axis=1).astype(bf16)
    scale = 1.0 / lseg
    mi_mat = (jnp.concatenate([c3, -s3], axis=1) * scale).transpose(0, 2, 1).astype(bf16)
    return (m_mat, mi_mat), (c, s), (nf, fc)


def _hyena_filters(lseg, w1, b1, w2, b2, freq, w3, b3, decay):
    hi = lax.Precision.HIGHEST
    t = jnp.arange(lseg, dtype=f32)
    tn = t / lseg
    bands = jnp.linspace(1e-4, HY_BANDS - 1, HY_BANDS, dtype=f32)
    ang = (2.0 * math.pi / lseg) * t[:, None] * bands[None, :]
    feat = jnp.concatenate([tn[:, None], jnp.cos(ang), -jnp.sin(ang)], axis=-1)
    hdn = jnp.sin(freq[0] * (jnp.dot(feat, w1, precision=hi) + b1))
    hdn = jnp.sin(freq[1] * (jnp.dot(hdn, w2, precision=hi) + b2))
    filt = (jnp.dot(hdn, w3, precision=hi) + b3).reshape(lseg, HY_ORDER, 2, HY_W)
    window = jnp.exp(-tn[:, None, None, None] * jnp.abs(decay)[None])
    return filt * window


def _hyena_kspec(lseg, cs, nf_fc, w1, b1, w2, b2, freq, w3, b3, decay, bias):
    c, s = cs
    nf, fc = nf_fc
    filt = _hyena_filters(lseg, w1, b1, w2, b2, freq, w3, b3, decay)
    out = []
    for o in range(HY_ORDER):
        hf = filt[:, o, 0].at[0].add(bias[o])
        hb = filt[:, o, 1].at[0].set(0.0)
        kr = _mm3(c, hf + hb).reshape(nf, fc, HY_W)
        ki = _mm3(s, hb - hf).reshape(nf, fc, HY_W)
        out.append(jnp.concatenate([kr, ki], axis=1))
    return out


def _lru_coef_kernel(xc_ref, w_ref, bias_ref, lam_ref, a_ref, b_ref):
    xc = xc_ref[...]
    z = jnp.dot(xc.astype(bf16), w_ref[...], preferred_element_type=f32) + bias_ref[...]
    g = jax.nn.sigmoid(z)
    for d in range(2):
        r = g[:, d * 2 * LRU_W:d * 2 * LRU_W + LRU_W]
        i = g[:, d * 2 * LRU_W + LRU_W:(d + 1) * 2 * LRU_W]
        log_a = -LRU_C * r * _softplus(-lam_ref[d:d + 1, :])
        th = jnp.tanh(log_a)
        one_minus_a2 = -2.0 * th / (1.0 - th)
        a_ref[d] = jnp.exp(log_a)
        b_ref[d] = jnp.sqrt(one_minus_a2) * i * xc


def _lru_coefs(xc2, w_all, bias_all, lam):
    r, w = xc2.shape
    tm = _pick(r, (512, 384, 256, 128))
    return pl.pallas_call(
        _lru_coef_kernel,
        out_shape=[jax.ShapeDtypeStruct((2, r, w), f32)] * 2,
        grid=(r // tm,),
        in_specs=[pl.BlockSpec((tm, w), lambda i: (i, 0)),
                  pl.BlockSpec((w, 4 * w), lambda i: (0, 0)),
                  pl.BlockSpec((1, 4 * w), lambda i: (0, 0)),
                  pl.BlockSpec((2, w), lambda i: (0, 0))],
        out_specs=[pl.BlockSpec((2, tm, w), lambda i: (0, i, 0))] * 2,
        compiler_params=_params(("parallel",), 40),
        name="lru_coefs",
    )(xc2, w_all, bias_all, lam)


def _lru_scan_kernel(af_ref, bf_ref, ab_ref, bb_ref, hf_ref, hb_ref, st_ref, *, blk):
    @pl.when(pl.program_id(1) == 0)
    def _():
        st_ref[...] = jnp.zeros_like(st_ref)

    def body(g, carry):
        hf, hb = carry
        base = pl.multiple_of(g * SUBLANES, SUBLANES)
        rbase = pl.multiple_of(blk - SUBLANES - g * SUBLANES, SUBLANES)
        for r in range(SUBLANES):
            hf = af_ref[0, 0, pl.ds(base + r, 1), :] * hf + bf_ref[0, 0, pl.ds(base + r, 1), :]
            hf_ref[0, pl.ds(base + r, 1), :] = hf
            rr = rbase + (SUBLANES - 1 - r)
            hb = ab_ref[0, 0, pl.ds(rr, 1), :] * hb + bb_ref[0, 0, pl.ds(rr, 1), :]
            hb_ref[0, pl.ds(rr, 1), :] = hb
        return hf, hb

    hf, hb = lax.fori_loop(0, blk // SUBLANES, body, (st_ref[0:1, :], st_ref[1:2, :]))
    st_ref[0:1, :] = hf
    st_ref[1:2, :] = hb


def _seg_orders(n_lat, n_ctx):
    def fwd(i):
        return jnp.where(i < n_ctx, n_lat + i, i - n_ctx)

    def bwd(i):
        return jnp.where(i < n_ctx, n_lat + n_ctx - 1 - i, n_lat - 1 - (i - n_ctx))

    return fwd, bwd


def _lru_scan(a4, b4, l_lat):
    _, nb, t_all, w = a4.shape
    lc = t_all - l_lat
    blk = _pick(math.gcd(l_lat, lc), (256, 128))
    fwd, bwd = _seg_orders(l_lat // blk, lc // blk)
    spec_f = pl.BlockSpec((1, 1, blk, w), lambda b, i: (0, b, fwd(i), 0))
    spec_b = pl.BlockSpec((1, 1, blk, w), lambda b, i: (1, b, bwd(i), 0))
    return pl.pallas_call(
        functools.partial(_lru_scan_kernel, blk=blk),
        out_shape=[jax.ShapeDtypeStruct((nb, t_all, w), f32)] * 2,
        grid=(nb, t_all // blk),
        in_specs=[spec_f, spec_f, spec_b, spec_b],
        out_specs=[pl.BlockSpec((1, blk, w), lambda b, i: (b, fwd(i), 0)),
                   pl.BlockSpec((1, blk, w), lambda b, i: (b, bwd(i), 0))],
        scratch_shapes=[pltpu.VMEM((SUBLANES, w), f32)],
        compiler_params=_params(("parallel", "arbitrary"), 32),
        name="lru_scan",
    )(a4, b4, a4, b4)


def _lru_finish_kernel(g_ref, hf_ref, hb_ref, o_ref):
    g = g_ref[...].astype(f32)
    gelu = 0.5 * g * (1.0 + jnp.tanh(math.sqrt(2.0 / math.pi) * (g + 0.044715 * (g * g * g))))
    o_ref[...] = (gelu * (hf_ref[...] + hb_ref[...])).astype(o_ref.dtype)


def _lru_finish(proj2, hf2, hb2):
    r, w = hf2.shape
    tm = _pick(r, (1024, 768, 512, 384, 256, 128))
    return pl.pallas_call(
        _lru_finish_kernel,
        out_shape=jax.ShapeDtypeStruct((r, w), bf16),
        grid=(r // tm,),
        in_specs=[pl.BlockSpec((tm, w), lambda i: (i, C_LRU_G // LRU_W)),
                  pl.BlockSpec((tm, w), lambda i: (i, 0)),
                  pl.BlockSpec((tm, w), lambda i: (i, 0))],
        out_specs=pl.BlockSpec((tm, w), lambda i: (i, 0)),
        compiler_params=_params(("parallel",), 32),
        name="lru_finish",
    )(proj2, hf2, hb2)


def _ssd_dir(d, x_ref, dc_ref, dr_ref, tri_ref, trit_ref, alog_l_ref, dtb_l_ref, alog_c_ref, dtb_c_ref,
             y_ref, st_ref):
    hi = lax.Precision.HIGHEST
    q = SSD_CHUNK
    p = SSD_HEAD_DIM
    ns = SSD_STATE
    per_g = SSD_HEADS // SSD_GROUPS
    xbc = x_ref[0]
    tri = tri_ref[d]
    mask = tri > 0.5
    dt_c = _softplus(dc_ref[0] + dtb_l_ref[...])
    dta_c = dt_c * (-jnp.exp(alog_l_ref[...]))
    cs_c = jnp.dot(tri, dta_c, precision=hi, preferred_element_type=f32)
    dt_r = _softplus(dr_ref[0] + dtb_c_ref[...])
    dta_r = dt_r * (-jnp.exp(alog_c_ref[...]))
    cs_r = jnp.dot(dta_r, trit_ref[d], precision=hi, preferred_element_type=f32)
    tot_r = jnp.sum(dta_r, axis=1, keepdims=True)
    ys = []
    for g in range(SSD_GROUPS):
        bm = xbc[:, SSD_W + g * ns:SSD_W + (g + 1) * ns]
        cm = xbc[:, SSD_W + SSD_GROUPS * ns + g * ns:SSD_W + SSD_GROUPS * ns + (g + 1) * ns]
        cmb = cm.astype(bf16)
        cb = lax.dot_general(cmb, bm.astype(bf16), (((1,), (1,)), ((), ())), preferred_element_type=f32)
        bm_t = bm.T
        for hg in range(per_g):
            h = g * per_g + hg
            hh = d * SSD_HEADS + h
            csc = cs_c[:, hh:hh + 1]
            csr = cs_r[hh:hh + 1, :]
            tot = tot_r[hh:hh + 1, :]
            decay = jnp.where(mask, jnp.exp(csc - csr), 0.0)
            xdt = (xbc[:, h * p:(h + 1) * p] * dt_c[:, hh:hh + 1]).astype(bf16)
            y_diag = jnp.dot((cb * decay).astype(bf16), xdt, preferred_element_type=f32)
            s_prev = st_ref[d, h]
            y_off = jnp.dot(cmb, s_prev.astype(bf16), preferred_element_type=f32) * jnp.exp(csc)
            bw_t = (bm_t * jnp.exp(tot - csr)).astype(bf16)
            st_ref[d, h] = jnp.exp(tot) * s_prev + jnp.dot(bw_t, xdt, preferred_element_type=f32)
            ys.append(y_diag + y_off)
    y_ref[0] = jnp.concatenate(ys, axis=1)


def _ssd_kernel(xf_ref, dcf_ref, drf_ref, xb_ref, dcb_ref, drb_ref, tri_ref, trit_ref,
                alog_l_ref, dtb_l_ref, alog_c_ref, dtb_c_ref, yf_ref, yb_ref, st_ref):
    @pl.when(pl.program_id(1) == 0)
    def _():
        st_ref[...] = jnp.zeros_like(st_ref)

    consts = (tri_ref, trit_ref, alog_l_ref, dtb_l_ref, alog_c_ref, dtb_c_ref)
    _ssd_dir(0, xf_ref, dcf_ref, drf_ref, *consts, yf_ref, st_ref)
    _ssd_dir(1, xb_ref, dcb_ref, drb_ref, *consts, yb_ref, st_ref)


def _ssd_scan(xbc, dt3, dt_t, a_log, dt_bias, l_lat):
    nb, t_all, _ = xbc.shape
    q = SSD_CHUNK
    lc = t_all - l_lat
    fwd, bwd = _seg_orders(l_lat // q, lc // q)
    low = jnp.tril(jnp.ones((q, q), f32))
    tri = jnp.stack([low, low.T])
    trit = jnp.stack([low.T, low])
    nh2 = 2 * SSD_HEADS
    alog_l = jnp.zeros((1, LANES), f32).at[0, :nh2].set(a_log.reshape(nh2))
    dtb_l = jnp.zeros((1, LANES), f32).at[0, :nh2].set(dt_bias.reshape(nh2))

    def specs(order):
        return [pl.BlockSpec((1, q, SSD_XBC), lambda b, i: (b, order(i), 0)),
                pl.BlockSpec((1, q, LANES), lambda b, i: (b, order(i), 0)),
                pl.BlockSpec((1, nh2, q), lambda b, i: (b, 0, order(i)))]

    def const(shape):
        return pl.BlockSpec(shape, lambda b, i: (0,) * len(shape))

    return pl.pallas_call(
        _ssd_kernel,
        out_shape=[jax.ShapeDtypeStruct((nb, t_all, SSD_W), f32)] * 2,
        grid=(nb, t_all // q),
        in_specs=specs(fwd) + specs(bwd) + [const((2, q, q)), const((2, q, q)), const((1, LANES)),
                                            const((1, LANES)), const((nh2, 1)), const((nh2, 1))],
        out_specs=[pl.BlockSpec((1, q, SSD_W), lambda b, i: (b, fwd(i), 0)),
                   pl.BlockSpec((1, q, SSD_W), lambda b, i: (b, bwd(i), 0))],
        scratch_shapes=[pltpu.VMEM((2, SSD_HEADS, SSD_STATE, SSD_HEAD_DIM), f32)],
        compiler_params=_params(("parallel", "arbitrary"), 32),
        name="ssd_scan",
    )(xbc, dt3, dt_t, xbc, dt3, dt_t, tri, trit, alog_l, dtb_l,
      a_log.reshape(nh2, 1), dt_bias.reshape(nh2, 1))


def _ssd_finish_kernel(yf_ref, yb_ref, xs_ref, z_ref, dsk_ref, g_ref, o_ref):
    y = yf_ref[...] + yb_ref[...] + dsk_ref[...] * xs_ref[...]
    y = y * _silu(z_ref[...].astype(f32))
    y = y * lax.rsqrt(jnp.mean(y * y, axis=-1, keepdims=True) + NORM_EPS) * g_ref[...]
    o_ref[...] = y.astype(o_ref.dtype)


def _ssd_finish(yf2, yb2, xbc2, proj2, d_skip, norm_g):
    r, w = yf2.shape
    tm = _pick(r, (1024, 768, 512, 384, 256, 128))
    row = lambda i: (i, 0)
    return pl.pallas_call(
        _ssd_finish_kernel,
        out_shape=jax.ShapeDtypeStruct((r, w), bf16),
        grid=(r // tm,),
        in_specs=[pl.BlockSpec((tm, w), row), pl.BlockSpec((tm, w), row), pl.BlockSpec((tm, w), row),
                  pl.BlockSpec((tm, w), lambda i: (i, C_SSD_Z // SSD_W)),
                  pl.BlockSpec((1, w), lambda i: (0, 0)), pl.BlockSpec((1, w), lambda i: (0, 0))],
        out_specs=pl.BlockSpec((tm, w), row),
        compiler_params=_params(("parallel",), 40),
        name="ssd_finish",
    )(yf2, yb2, xbc2, proj2, jnp.repeat(d_skip, SSD_HEAD_DIM).reshape(1, w), norm_g.reshape(1, w))


def _router_kernel(f_ref, w_ref, b_ref, idx_ref, wt_ref):
    hi = lax.Precision.HIGHEST
    logits = jnp.dot(f_ref[...], w_ref[...], precision=hi, preferred_element_type=f32) + b_ref[...]
    lane = lax.broadcasted_iota(jnp.int32, logits.shape, 1)
    neg = jnp.float32(-jnp.inf)
    logits = jnp.where(lane < N_EXPERTS, logits, neg)
    v1 = jnp.max(logits, axis=-1, keepdims=True)
    i1 = jnp.min(jnp.where(logits == v1, lane, GATE_LANES), axis=-1, keepdims=True)
    rest = jnp.where(lane == i1, neg, logits)
    v2 = jnp.max(rest, axis=-1, keepdims=True)
    i2 = jnp.min(jnp.where(rest == v2, lane, GATE_LANES), axis=-1, keepdims=True)
    e2 = jnp.exp(v2 - v1)
    w1 = 1.0 / (1.0 + e2)
    idx_ref[...] = jnp.where(lane == 0, i1, jnp.where(lane == 1, i2, 0))
    wt_ref[...] = jnp.where(lane == 0, w1, jnp.where(lane == 1, e2 * w1, 0.0))


def _router(fin, w_router, b_router):
    r, d = fin.shape
    tm = _pick(r, (512, 384, 256, 128))
    wp = jnp.zeros((d, GATE_LANES), f32).at[:, :N_EXPERTS].set(w_router)
    bp = jnp.zeros((1, GATE_LANES), f32).at[0, :N_EXPERTS].set(b_router)
    return pl.pallas_call(
        _router_kernel,
        out_shape=[jax.ShapeDtypeStruct((r, GATE_LANES), jnp.int32), jax.ShapeDtypeStruct((r, GATE_LANES), f32)],
        grid=(r // tm,),
        in_specs=[pl.BlockSpec((tm, d), lambda i: (i, 0)),
                  pl.BlockSpec((d, GATE_LANES), lambda i: (0, 0)),
                  pl.BlockSpec((1, GATE_LANES), lambda i: (0, 0))],
        out_specs=[pl.BlockSpec((tm, GATE_LANES), lambda i: (i, 0))] * 2,
        compiler_params=_params(("parallel",), 40),
        name="router",
    )(fin, wp, bp)


MOE_TG = 512


def _route_meta(e_idx, rg):
    n_assign = e_idx.size
    e_flat = e_idx.reshape(n_assign)
    onehot = (e_flat[:, None] == jnp.arange(N_EXPERTS, dtype=jnp.int32)[None, :]).astype(jnp.int32)
    csum = jnp.cumsum(onehot, axis=0)
    counts = csum[-1]
    rank = jnp.take_along_axis(csum, e_flat[:, None], axis=1)[:, 0] - 1
    gsize = ((counts + MOE_TG - 1) // MOE_TG) * MOE_TG
    ends = jnp.cumsum(gsize)
    pos = (ends - gsize)[e_flat] + rank
    tile_start = jnp.arange(rg // MOE_TG, dtype=jnp.int32) * MOE_TG
    tile_expert = jnp.minimum(jnp.sum((tile_start[:, None] >= ends[None, :]).astype(jnp.int32), axis=1),
                              N_EXPERTS - 1).astype(jnp.int32)
    n_used = (ends[-1:] // MOE_TG).astype(jnp.int32)
    src = jnp.zeros((rg,), jnp.int32).at[pos].set(jnp.arange(n_assign, dtype=jnp.int32) // TOP_K)
    return pos.reshape(e_idx.shape).astype(jnp.int32), src, tile_expert, n_used


def _row_gather_kernel(src_ref, x_hbm, o_ref, buf_ref, sem):
    rows = buf_ref.shape[0]

    def issue(r, carry):
        pltpu.make_async_copy(x_hbm.at[pl.ds(src_ref[r], 1), :], buf_ref.at[pl.ds(r, 1), :], sem).start()
        return carry

    lax.fori_loop(0, rows, issue, 0, unroll=8)
    pltpu.make_async_copy(x_hbm.at[pl.ds(0, rows), :], buf_ref, sem).wait()
    o_ref[...] = buf_ref[...].astype(o_ref.dtype)


def _row_gather(x2, src, out_dtype):
    rg = src.shape[0]
    d = x2.shape[1]
    return pl.pallas_call(
        _row_gather_kernel,
        out_shape=jax.ShapeDtypeStruct((rg, d), out_dtype),
        grid=(rg // MOE_TG,),
        in_specs=[pl.BlockSpec((MOE_TG,), lambda i: (i,), memory_space=pltpu.SMEM),
                  pl.BlockSpec(memory_space=pl.ANY)],
        out_specs=pl.BlockSpec((MOE_TG, d), lambda i: (i, 0)),
        scratch_shapes=[pltpu.VMEM((MOE_TG, d), x2.dtype), pltpu.SemaphoreType.DMA(())],
        compiler_params=_params(("arbitrary",), 32),
        name="moe_gather",
    )(src, x2)


def _cast_on_expert_change(te_ref, used, pairs):
    i = pl.program_id(1)
    changed = jnp.logical_or(i == 0, te_ref[i] != te_ref[jnp.maximum(i - 1, 0)])

    @pl.when(jnp.logical_and(used, changed))
    def _():
        for src, dst in pairs:
            dst[...] = src[0].astype(bf16)


def _gmm_swiglu_kernel(te_ref, nu_ref, a_ref, w1_ref, w3_ref, o_ref, w1b_ref, w3b_ref):
    used = pl.program_id(1) < nu_ref[0]
    _cast_on_expert_change(te_ref, used, ((w1_ref, w1b_ref), (w3_ref, w3b_ref)))

    @pl.when(used)
    def _():
        a = a_ref[...]
        g = jnp.dot(a, w1b_ref[...], preferred_element_type=f32)
        u = jnp.dot(a, w3b_ref[...], preferred_element_type=f32)
        o_ref[...] = (_silu(g) * u).astype(o_ref.dtype)

    @pl.when(jnp.logical_not(used))
    def _():
        o_ref[...] = jnp.zeros_like(o_ref)


def _gmm_out_kernel(te_ref, nu_ref, a_ref, w_ref, o_ref, wb_ref):
    used = pl.program_id(1) < nu_ref[0]
    _cast_on_expert_change(te_ref, used, ((w_ref, wb_ref),))

    @pl.when(used)
    def _():
        o_ref[...] = jnp.dot(a_ref[...], wb_ref[...], preferred_element_type=f32)

    @pl.when(jnp.logical_not(used))
    def _():
        o_ref[...] = jnp.zeros_like(o_ref)


def _used(i, nu):
    return jnp.minimum(i, nu[0] - 1)


def _gmm_swiglu(xs, w1, w3, layer, tile_expert, n_used, tn=1024):
    rg, k = xs.shape
    n = w1.shape[-1]
    wspec = pl.BlockSpec((None, 1, k, tn), lambda j, i, te, nu: (layer, te[_used(i, nu)], 0, j),
                         pipeline_mode=pl.Buffered(1))
    return pl.pallas_call(
        _gmm_swiglu_kernel,
        out_shape=jax.ShapeDtypeStruct((rg, n), bf16),
        grid_spec=pltpu.PrefetchScalarGridSpec(
            num_scalar_prefetch=2, grid=(n // tn, rg // MOE_TG),
            in_specs=[pl.BlockSpec((MOE_TG, k), lambda j, i, te, nu: (_used(i, nu), 0)), wspec, wspec],
            out_specs=pl.BlockSpec((MOE_TG, tn), lambda j, i, te, nu: (i, j)),
            scratch_shapes=[pltpu.VMEM((k, tn), bf16)] * 2),
        compiler_params=_params(("parallel", "arbitrary"), 56),
        name="moe_up",
    )(tile_expert, n_used, xs, w1, w3)


def _gmm_out(hmid, w2, layer, tile_expert, n_used, tn=1024):
    rg, k = hmid.shape
    n = w2.shape[-1]
    return pl.pallas_call(
        _gmm_out_kernel,
        out_shape=jax.ShapeDtypeStruct((rg, n), f32),
        grid_spec=pltpu.PrefetchScalarGridSpec(
            num_scalar_prefetch=2, grid=(n // tn, rg // MOE_TG),
            in_specs=[pl.BlockSpec((MOE_TG, k), lambda j, i, te, nu: (_used(i, nu), 0)),
                      pl.BlockSpec((None, 1, k, tn), lambda j, i, te, nu: (layer, te[_used(i, nu)], 0, j),
                                   pipeline_mode=pl.Buffered(1))],
            out_specs=pl.BlockSpec((MOE_TG, tn), lambda j, i, te, nu: (i, j)),
            scratch_shapes=[pltpu.VMEM((k, tn), bf16)]),
        compiler_params=_params(("parallel", "arbitrary"), 56),
        name="moe_down",
    )(tile_expert, n_used, hmid, w2)


def _combine_kernel(p0_ref, p1_ref, ys_hbm, x_ref, m_ref, wt_ref, o_ref, b0_ref, b1_ref, sems, *, k_mod):
    rows = b0_ref.shape[0]

    def issue(r, carry):
        pltpu.make_async_copy(ys_hbm.at[pl.ds(p0_ref[r], 1), :], b0_ref.at[pl.ds(r, 1), :], sems.at[0]).start()
        pltpu.make_async_copy(ys_hbm.at[pl.ds(p1_ref[r], 1), :], b1_ref.at[pl.ds(r, 1), :], sems.at[1]).start()
        return carry

    lax.fori_loop(0, rows, issue, 0, unroll=8)
    pltpu.make_async_copy(ys_hbm.at[pl.ds(0, rows), :], b0_ref, sems.at[0]).wait()
    pltpu.make_async_copy(ys_hbm.at[pl.ds(0, rows), :], b1_ref, sems.at[1]).wait()
    mixed = wt_ref[:, 0:1] * b0_ref[...] + wt_ref[:, 1:2] * b1_ref[...]
    o_ref[...] = x_ref[...] + m_ref[0, k_mod:k_mod + 1, :] * mixed


def _moe_combine(ys, pos, wts, x2, modtab, nb, t_all, l_lat, k_mod):
    r, d = x2.shape
    tm = _pick(math.gcd(l_lat, t_all - l_lat), (256, 128))
    tpb = t_all // tm
    n_lat_tiles = l_lat // tm
    sspec = pl.BlockSpec((tm,), lambda i: (i,), memory_space=pltpu.SMEM)
    return pl.pallas_call(
        functools.partial(_combine_kernel, k_mod=k_mod),
        out_shape=jax.ShapeDtypeStruct((r, d), f32),
        grid=(r // tm,),
        in_specs=[sspec, sspec, pl.BlockSpec(memory_space=pl.ANY),
                  pl.BlockSpec((tm, d), lambda i: (i, 0)),
                  pl.BlockSpec((1, 6, d), lambda i: (jnp.where(i % tpb >= n_lat_tiles, nb, i // tpb), 0, 0)),
                  pl.BlockSpec((tm, GATE_LANES), lambda i: (i, 0))],
        out_specs=pl.BlockSpec((tm, d), lambda i: (i, 0)),
        scratch_shapes=[pltpu.VMEM((tm, d), f32), pltpu.VMEM((tm, d), f32), pltpu.SemaphoreType.DMA((2,))],
        input_output_aliases={3: 0},
        compiler_params=_params(("arbitrary",), 32),
        name="moe_combine",
    )(pos[:, 0], pos[:, 1], ys, x2, modtab, wts)


def _moe(fin32, xall, w_router, b_router, w1, w3, w2, layer, modtab, nb, t_all, l_lat):
    rows = fin32.shape[0]
    rg = TOP_K * rows + N_EXPERTS * MOE_TG
    idx, wts = _router(fin32, w_router, b_router)
    pos, src, tile_expert, n_used = _route_meta(idx[:, :TOP_K], rg)
    xs = _row_gather(fin32, src, bf16)
    hmid = _gmm_swiglu(xs, w1, w3, layer, tile_expert, n_used)
    ys = _gmm_out(hmid, w2, layer, tile_expert, n_used)
    return _moe_combine(ys, pos, wts, xall, modtab, nb, t_all, l_lat, 5)


def _final_norm_kernel(x_ref, g_ref, o_ref):
    x = x_ref[0]
    o_ref[0] = x * lax.rsqrt(jnp.mean(x * x, axis=-1, keepdims=True) + NORM_EPS) * g_ref[...]


def _final_norm(x3, gain, l_lat):
    nb, t_all, d = x3.shape
    tr = _pick(math.gcd(l_lat, t_all), (512, 256, 128))
    return pl.pallas_call(
        _final_norm_kernel,
        out_shape=jax.ShapeDtypeStruct((nb, l_lat, d), f32),
        grid=(nb, l_lat // tr),
        in_specs=[pl.BlockSpec((1, tr, d), lambda b, j: (b, j, 0)),
                  pl.BlockSpec((1, d), lambda b, j: (0, 0))],
        out_specs=pl.BlockSpec((1, tr, d), lambda b, j: (b, j, 0)),
        compiler_params=_params(("parallel", "parallel"), 32),
        name="final_norm",
    )(x3, gain.reshape(1, d))


def _split_w_in(w):
    dt0 = C_SSD_XBC + SSD_XBC
    dt1 = dt0 + 2 * SSD_HEADS
    w_main = jnp.concatenate([w[:, :dt0], w[:, dt1:]], axis=1)
    w_dt = jnp.pad(w[:, dt0:dt1], ((0, 0), (0, LANES - (dt1 - dt0))))
    return w_main, w_dt


def _block_diag(w):
    h, bw, _ = w.shape
    eye = jnp.eye(h, dtype=w.dtype)
    return (eye[:, None, :, None] * w[:, :, None, :]).reshape(h * bw, h * bw)


def kernel(x, c, ctx, c_ctx, norm_mix, norm_ffn, ada_w, ada_b, w_in, hy_conv_w, hy_conv_b, hy_w1, hy_b1, hy_w2, hy_b2, hy_freq, hy_w3, hy_b3, hy_decay, hy_bias, lru_conv_w, lru_conv_b, lru_wa, lru_ba, lru_wx, lru_bx, lru_lambda, ssd_conv_w, ssd_conv_b, ssd_a_log, ssd_dt_bias, ssd_d, ssd_norm, att_q_norm, att_k_norm, w_branch, w_gate, b_gate, w_out, ffn_w1, ffn_w3, ffn_w2, moe_router, moe_router_b, moe_w1, moe_w3, moe_w2, norm_final):
    nb, l_lat, d = x.shape
    lc = ctx.shape[1]
    t_all = l_lat + lc
    rows = nb * t_all
    depth = norm_mix.shape[0]
    tm = _pick(t_all, (768, 384, 256, 128))

    mod_rows = -(-(nb + 1) // SUBLANES) * SUBLANES
    cvec = jnp.zeros((mod_rows, d), f32).at[:nb].set(c).at[nb].set(c_ctx)
    mods = _mods(cvec, ada_w, ada_b).reshape(depth, mod_rows, 6, d)

    cos2, sin2 = _rope_tables(l_lat, lc)
    dft_l, cs_l, nf_l = _dft_mats(l_lat)
    dft_c, cs_c, nf_c = _dft_mats(lc)

    xall = jnp.concatenate([x, ctx], axis=1).reshape(rows, d)

    for l in range(depth):
        with_ctx = l < depth - 1
        modtab = mods[l]
        h = _normmod(xall, norm_mix[l], modtab, nb, t_all, l_lat, 0, 1)
        w_main, w_dt = _split_w_in(w_in[l])
        proj2 = _mm(h, w_main, bf16, t_all, 512)
        proj = proj2.reshape(nb, t_all, PROJ_W)
        dt3 = _mm(h, w_dt, f32, t_all, LANES).reshape(nb, t_all, LANES)

        hy_args = (hy_w1[l], hy_b1[l], hy_w2[l], hy_b2[l], hy_freq[l], hy_w3[l], hy_b3[l], hy_decay[l], hy_bias[l])
        hu = _dwconv(proj, C_HY, HY_IN, hy_conv_w[l], hy_conv_b[l], (HY_SHORT - 1) // 2, l_lat)
        k_l = _hyena_kspec(l_lat, cs_l, nf_l, *hy_args)
        z_l = _hyconv(hu, (0, 0), hu, (0, 1), l_lat, dft_l, k_l[0])
        hy_l = _hyconv(z_l, (0, 0), hu, (0, 2), l_lat, dft_l, k_l[1])
        if with_ctx:
            k_c = _hyena_kspec(lc, cs_c, nf_c, *hy_args)
            z_c = _hyconv(hu, (l_lat // lc, 0), hu, (l_lat // lc, 1), lc, dft_c, k_c[0])
            hy_c = _hyconv(z_c, (0, 0), hu, (l_lat // lc, 2), lc, dft_c, k_c[1])
        else:
            hy_c = jnp.zeros((nb, lc, HY_W), bf16)
        y_hy = jnp.concatenate([hy_l, hy_c], axis=1).reshape(rows, HY_W)

        xc = _dwconv(proj, C_LRU_X, LRU_W, lru_conv_w[l], lru_conv_b[l], LRU_CONV // 2, l_lat)
        w_all = jnp.concatenate([_block_diag(lru_wa[l, 0]), _block_diag(lru_wx[l, 0]),
                                 _block_diag(lru_wa[l, 1]), _block_diag(lru_wx[l, 1])], axis=1).astype(bf16)
        bias_all = jnp.concatenate([lru_ba[l, 0].reshape(-1), lru_bx[l, 0].reshape(-1),
                                    lru_ba[l, 1].reshape(-1), lru_bx[l, 1].reshape(-1)]).reshape(1, 4 * LRU_W)
        a2, b2 = _lru_coefs(xc.reshape(rows, LRU_W), w_all, bias_all, lru_lambda[l])
        hf, hb = _lru_scan(a2.reshape(2, nb, t_all, LRU_W), b2.reshape(2, nb, t_all, LRU_W), l_lat)
        y_lru = _lru_finish(proj2, hf.reshape(rows, LRU_W), hb.reshape(rows, LRU_W))

        xbc = _dwconv(proj, C_SSD_XBC, SSD_XBC, ssd_conv_w[l], ssd_conv_b[l], SSD_CONV // 2, l_lat, act=True)
        dt_t = dt3[:, :, :2 * SSD_HEADS].transpose(0, 2, 1)
        yf, yb = _ssd_scan(xbc, dt3, dt_t, ssd_a_log[l], ssd_dt_bias[l], l_lat)
        y_ssd = _ssd_finish(yf.reshape(rows, SSD_W), yb.reshape(rows, SSD_W), xbc.reshape(rows, SSD_XBC),
                            proj2, ssd_d[l], ssd_norm[l])

        y_att = _attention(proj, cos2, sin2, att_q_norm[l], att_k_norm[l], l_lat).reshape(rows, ATT_Q)

        merged = _merge(h, [y_hy, y_lru, y_ssd, y_att], w_gate, b_gate[l], w_branch, l, tm)
        xall = _mm_resid(merged, w_out, xall, modtab, nb, t_all, l_lat, 2, tm, 1024, lead=(l,))

        t_ffn = t_all
        if not with_ctx:
            xall = xall.reshape(nb, t_all, d)[:, :l_lat].reshape(nb * l_lat, d)
            t_ffn = l_lat
        j = l // 2
        if l % 2 == 0:
            fin = _normmod(xall, norm_ffn[l], modtab, nb, t_ffn, l_lat, 3, 4)
            padc = FF_DENSE_PAD - FF_DENSE
            w1 = jnp.pad(ffn_w1[j], ((0, 0), (0, padc)))
            w3 = jnp.pad(ffn_w3[j], ((0, 0), (0, padc)))
            w2 = jnp.pad(ffn_w2[j], ((0, padc), (0, 0)))
            hmid = _mm_swiglu(fin, w1, w3, _pick(t_ffn, (768, 512, 384, 256, 128)), 512)
            xall = _mm_resid(hmid, w2, xall, modtab, nb, t_ffn, l_lat, 5, _pick(t_ffn, (384, 256, 128)), 512, vmem=56)
        else:
            fin32 = _normmod(xall, norm_ffn[l], modtab, nb, t_ffn, l_lat, 3, 4, out_dtype=f32)
            xall = _moe(fin32, xall, moe_router[j], moe_router_b[j], moe_w1, moe_w3, moe_w2, j,
                        modtab, nb, t_ffn, l_lat)

    return _final_norm(xall.reshape(nb, t_ffn, d), norm_final, l_lat)
```
